```python
import math
import jax, jax.numpy as jnp
from jax import lax
import numpy as np

D_MODEL = 1024
BATCH = 16
SEQ = 2048
DEPTH = 4

CTX_LEN = 256
GRID_W = 64

D_HY = 384
D_S5 = 384
D_ML = 384
N_BRANCH = 3

SHORT_CONV = 3

HY_ORDER = 2
HY_EMB = 33
HY_BANDS = (HY_EMB - 1) // 2
HY_FILTER_HIDDEN = 64
HY_DECAY_TARGET = 1e-2
HY_FAST_DECAY_PCT = 0.3
HY_SLOW_DECAY_PCT = 1.5
HY_MIN_DECAY = math.log(HY_DECAY_TARGET) / HY_SLOW_DECAY_PCT
HY_MAX_DECAY = math.log(HY_DECAY_TARGET) / HY_FAST_DECAY_PCT

S5_GROUP = 16
S5_GROUPS = D_S5 // S5_GROUP
S5_STATE = 64
S5_DT_MIN = 1e-3
S5_DT_MAX = 1e-1

ML_HEADS = 4
ML_HEAD_DIM = D_ML // ML_HEADS
ML_CHUNK = 64
NEG = -1e30

N_EXPERTS = 32
TOP_K = 4
D_FF_EXPERT = 512
SWIGLU_LIMIT = 7.0
SWIGLU_ALPHA = 1.702
MOE_BLOCK = 256

NORM_EPS = 1e-6

COL_SIZES = (D_S5, 2 * D_ML, D_ML, 4 * ML_HEADS, (1 + HY_ORDER) * D_HY, D_ML, N_BRANCH * D_MODEL)
REC_COLS = D_S5 + 3 * D_ML + 4 * ML_HEADS
N_IN = REC_COLS + (1 + HY_ORDER) * D_HY + D_ML + N_BRANCH * D_MODEL

kernel_name = 'hybrid_hyena_s5_mlstm_moe_prefix_dit'


def rmsnorm(x, g):
    xf = x.astype(jnp.float32)
    y = xf * lax.rsqrt(jnp.mean(xf * xf, axis=-1, keepdims=True) + NORM_EPS)
    return (y * g.astype(jnp.float32)).astype(x.dtype)


def modulate(h, shift, scale):
    return h * (1 + scale) + shift


def split_cols(p, sizes):
    out, start = [], 0
    for s in sizes:
        out.append(p[..., start:start + s])
        start += s
    return out


def short_conv(u, w, b, grid):
    bsz, _, ch = u.shape
    n_rows, row_len = grid
    pad = SHORT_CONV // 2
    r = jnp.pad(u.reshape(bsz, n_rows, row_len, ch), ((0, 0), (0, 0), (pad, pad), (0, 0)))
    y = b + r[:, :, 0:row_len] * w[0]
    for j in range(1, SHORT_CONV):
        y = y + r[:, :, j:j + row_len] * w[j]
    return y.reshape(bsz, n_rows * row_len, ch)


def hyena_filters(seq_len, w1, b1, w2, b2, w3, b3, sin_freq):
    t = jnp.linspace(0.0, 1.0, seq_len, dtype=jnp.float32)[:, None]
    w = 2.0 * math.pi * jnp.arange(seq_len, dtype=jnp.float32)[:, None] / seq_len
    f = jnp.linspace(1e-4, HY_BANDS - 1, HY_BANDS, dtype=jnp.float32)[None, :]
    z = jnp.concatenate([t, jnp.cos(f * w), -jnp.sin(f * w)], axis=-1)
    hdn = jnp.sin(sin_freq[0] * (z @ w1 + b1))
    hdn = jnp.sin(sin_freq[1] * (hdn @ w2 + b2))
    hf = (hdn @ w3 + b3).astype(jnp.float32).reshape(seq_len, HY_ORDER, 2, D_HY)
    deltas = jnp.abs(jnp.linspace(HY_MIN_DECAY, HY_MAX_DECAY, D_HY, dtype=jnp.float32))
    decay = jnp.exp(-t * deltas)
    return hf * decay[:, None, None, :]


def bidir_fftconv(u, h_fwd, h_bwd, skip):
    seq_len = u.shape[1]
    n_fft = 2 * seq_len
    k = jnp.concatenate([h_fwd, jnp.zeros_like(h_fwd[:1]), h_bwd[:0:-1]], axis=0)
    k = k / jnp.sum(jnp.abs(k), axis=0, keepdims=True)
    uf = u.astype(jnp.float32)
    spec = jnp.fft.rfft(uf, n=n_fft, axis=1) * jnp.fft.rfft(k, n=n_fft, axis=0)[None]
    y = jnp.fft.irfft(spec, n=n_fft, axis=1)[:, :seq_len]
    return (y + uf * skip.astype(jnp.float32)).astype(u.dtype)


def hyena_mixer(u, conv_w, conv_b, filter_params, bias, grid):
    filt = hyena_filters(u.shape[1], *filter_params)
    parts = jnp.split(short_conv(u, conv_w, conv_b, grid), 1 + HY_ORDER, axis=-1)
    z = parts[0]
    for o in range(HY_ORDER):
        z = parts[1 + o] * bidir_fftconv(z, filt[:, o, 0], filt[:, o, 1], bias[o])
    return z


def s5_discretize(lam_re, lam_im, log_dt, b_mat):
    lam = lax.complex(lam_re.astype(jnp.float32), lam_im.astype(jnp.float32))
    dt = jnp.exp(log_dt.astype(jnp.float32))[:, None]
    lam_bar = jnp.exp(lam * dt)
    b_bar = ((lam_bar - 1.0) / lam)[..., None] * b_mat
    return lam_bar, b_bar


def linear_scan(bu, lam_bar, x0):
    bu = bu.at[:, 0].add(lam_bar * x0)
    a = jnp.broadcast_to(lam_bar, (1, bu.shape[1]) + lam_bar.shape)

    def combine(left, right):
        return (left[0] * right[0], right[0] * left[1] + right[1])

    _, xs = lax.associative_scan(combine, (a, bu), axis=1)
    return xs


def s5_mixer(u_l, u_c, lam_re, lam_im, log_dt, b_re, b_im, c_re, c_im, d_skip, glu_w, need_ctx):
    bsz, seq_len, _ = u_l.shape
    ctx_len = u_c.shape[1]
    ul = u_l.astype(jnp.float32).reshape(bsz, seq_len, S5_GROUPS, S5_GROUP)
    uc = u_c.astype(jnp.float32).reshape(bsz, ctx_len, S5_GROUPS, S5_GROUP)
    b_mat = lax.complex(b_re.astype(jnp.float32), b_im.astype(jnp.float32))
    c_mat = lax.complex(c_re.astype(jnp.float32), c_im.astype(jnp.float32))
    dsk = d_skip.astype(jnp.float32).reshape(S5_GROUPS, S5_GROUP)
    x0 = jnp.zeros((bsz, S5_GROUPS, S5_STATE), jnp.complex64)
    y_l = dsk * ul
    y_c = dsk * uc if need_ctx else None
    for d in range(2):
        flip = (lambda t: jnp.flip(t, axis=1)) if d == 1 else (lambda t: t)
        lam_bar, b_bar = s5_discretize(lam_re[d], lam_im[d], log_dt[d], b_mat)
        xs_c = linear_scan(flip(jnp.einsum('blgn,gpn->blgp', uc.astype(jnp.complex64), b_bar)), lam_bar, x0)
        xs_l = flip(linear_scan(flip(jnp.einsum('blgn,gpn->blgp', ul.astype(jnp.complex64), b_bar)),
                                lam_bar, xs_c[:, -1]))
        y_l = y_l + jnp.real(jnp.einsum('blgp,gnp->blgn', xs_l, c_mat))
        if need_ctx:
            y_c = y_c + jnp.real(jnp.einsum('blgp,gnp->blgn', flip(xs_c), c_mat))

    def half_glu(y, dtype):
        g = jax.nn.gelu(y.reshape(bsz, -1, D_S5))
        return (g * jax.nn.sigmoid(g @ glu_w)).astype(dtype)

    return half_glu(y_l, u_l.dtype), (half_glu(y_c, u_c.dtype) if need_ctx else None)


def mlstm_chunkwise(q, k, v, log_i, log_f, state0, with_output):
    bsz, nh, seq_len, dh = q.shape
    nc = seq_len // ML_CHUNK
    blk = lambda t: t.reshape(bsz, nh, nc, ML_CHUNK, *t.shape[3:])
    qc, kc, vc, ic, fc = blk(q), blk(k), blk(v), blk(log_i), blk(log_f)
    b = jnp.cumsum(fc, axis=-1)
    b_end = b[..., -1]
    g = b_end[..., None] - b + ic

    def chunk_step(carry, inp):
        c_mem, n_mem, m_mem = carry
        k_t, v_t, g_t, be = inp
        m_new = jnp.maximum(be + m_mem, jnp.max(g_t, axis=-1))
        decay = jnp.exp(be + m_mem - m_new)
        w = jnp.exp(g_t - m_new[..., None])
        c_new = decay[..., None, None] * c_mem + jnp.einsum('bhs,bhsd,bhse->bhde', w, v_t, k_t)
        n_new = decay[..., None] * n_mem + jnp.einsum('bhs,bhse->bhe', w, k_t)
        return (c_new, n_new, m_new), (c_mem, n_mem, m_mem)

    to_scan = lambda t: jnp.moveaxis(t, 2, 0)
    final, (c_prev, n_prev, m_prev) = lax.scan(
        chunk_step, state0, (to_scan(kc), to_scan(vc), to_scan(g), to_scan(b_end)))
    if not with_output:
        return None, final
    c_prev = jnp.moveaxis(c_prev, 0, 2)
    n_prev = jnp.moveaxis(n_prev, 0, 2)
    m_prev = jnp.moveaxis(m_prev, 0, 2)
    lower = jnp.tril(jnp.ones((ML_CHUNK, ML_CHUNK), dtype=bool))
    a = b + m_prev[..., None]
    dmat = jnp.where(lower, b[..., :, None] - b[..., None, :] + ic[..., None, :], NEG)
    m_out = jnp.maximum(a, jnp.max(dmat, axis=-1))
    wmat = jnp.exp(dmat - m_out[..., None])
    wa = jnp.exp(a - m_out)
    s = jnp.einsum('bhctd,bhcsd->bhcts', qc, kc) * wmat
    num = jnp.einsum('bhcts,bhcsd->bhctd', s, vc) + wa[..., None] * jnp.einsum('bhcde,bhcte->bhctd', c_prev, qc)
    den = jnp.sum(s, axis=-1) + wa * jnp.einsum('bhce,bhcte->bhct', n_prev, qc)
    h = num / jnp.maximum(jnp.abs(den), jnp.exp(-m_out))[..., None]
    return h.reshape(bsz, nh, seq_len, dh), final


def mlstm_heads(qk, v, gates, gate_b, conv_w, conv_b, grid):
    bsz, seq_len, _ = v.shape
    qk = jax.nn.silu(short_conv(qk, conv_w, conv_b, grid)).astype(jnp.float32)
    q, k = jnp.split(qk, 2, axis=-1)
    to_heads = lambda t: t.reshape(bsz, seq_len, ML_HEADS, ML_HEAD_DIM).transpose(0, 2, 1, 3)
    g = (gates + gate_b).astype(jnp.float32).reshape(bsz, seq_len, 4, ML_HEADS).transpose(2, 0, 3, 1)
    return to_heads(q), to_heads(k) * ML_HEAD_DIM ** -0.5, to_heads(v.astype(jnp.float32)), g


def mlstm_head_out(h, o, norm_g):
    bsz, _, seq_len, _ = h.shape
    h = h.transpose(0, 2, 1, 3)
    h = h * lax.rsqrt(jnp.mean(h * h, axis=-1, keepdims=True) + NORM_EPS)
    h = h.reshape(bsz, seq_len, D_ML) * norm_g.astype(jnp.float32)
    return (h * jax.nn.sigmoid(o.astype(jnp.float32))).astype(o.dtype)


def mlstm_mixer(qk_l, v_l, gt_l, o_l, qk_c, v_c, gt_c, o_c, gate_b, conv_w, conv_b, norm_g,
                lat_grid, ctx_grid, need_ctx):
    ql, kl, vl, gl = mlstm_heads(qk_l, v_l, gt_l, gate_b, conv_w, conv_b, lat_grid)
    qc, kc, vc, gc = mlstm_heads(qk_c, v_c, gt_c, gate_b, conv_w, conv_b, ctx_grid)
    bsz = ql.shape[0]
    state0 = (jnp.zeros((bsz, ML_HEADS, ML_HEAD_DIM, ML_HEAD_DIM), jnp.float32),
              jnp.zeros((bsz, ML_HEADS, ML_HEAD_DIM), jnp.float32),
              jnp.full((bsz, ML_HEADS), NEG, jnp.float32))
    h_l = jnp.zeros_like(ql)
    h_c = jnp.zeros_like(qc) if need_ctx else None
    for d in range(2):
        flip = (lambda t: jnp.flip(t, axis=2)) if d == 1 else (lambda t: t)
        hc_d, state_c = mlstm_chunkwise(flip(qc), flip(kc), flip(vc), flip(gc[2 * d]),
                                        flip(jax.nn.log_sigmoid(gc[2 * d + 1])), state0, need_ctx)
        hl_d, _ = mlstm_chunkwise(flip(ql), flip(kl), flip(vl), flip(gl[2 * d]),
                                  flip(jax.nn.log_sigmoid(gl[2 * d + 1])), state_c, True)
        h_l = h_l + flip(hl_d)
        if need_ctx:
            h_c = h_c + flip(hc_d)
    out_l = mlstm_head_out(h_l, o_l, norm_g)
    out_c = mlstm_head_out(h_c, o_c, norm_g) if need_ctx else None
    return out_l, out_c


def gated_merge(hy, s5, ml, mg, w_br_hy, w_br_s5, w_br_ml, w_out):
    g_hy, g_s5, g_ml = jnp.split(jax.nn.sigmoid(mg), N_BRANCH, axis=-1)
    y = g_hy * (hy @ w_br_hy) + g_s5 * (s5 @ w_br_s5) + g_ml * (ml @ w_br_ml)
    return y @ w_out


def token_mixer(h_l, h_c, lat_grid, ctx_grid, need_ctx, w_in, hy_conv_w, hy_conv_b, hy_filter_params, hy_bias,
                s5_params, ml_conv_w, ml_conv_b, ml_gate_b, ml_norm_g, w_br_hy, w_br_s5, w_br_ml, w_out):
    s5_l, qk_l, v_l, gt_l, hy_l, o_l, mg_l = split_cols(h_l @ w_in, COL_SIZES)
    if need_ctx:
        s5_c, qk_c, v_c, gt_c, hy_c, o_c, mg_c = split_cols(h_c @ w_in, COL_SIZES)
    else:
        s5_c, qk_c, v_c, gt_c = split_cols(h_c @ w_in[:, :REC_COLS], COL_SIZES[:4])
        hy_c = o_c = mg_c = None
    hy_out_l = hyena_mixer(hy_l, hy_conv_w, hy_conv_b, hy_filter_params, hy_bias, lat_grid)
    s5_out_l, s5_out_c = s5_mixer(s5_l, s5_c, *s5_params, need_ctx)
    ml_out_l, ml_out_c = mlstm_mixer(qk_l, v_l, gt_l, o_l, qk_c, v_c, gt_c, o_c, ml_gate_b, ml_conv_w,
                                     ml_conv_b, ml_norm_g, lat_grid, ctx_grid, need_ctx)
    y_l = gated_merge(hy_out_l, s5_out_l, ml_out_l, mg_l, w_br_hy, w_br_s5, w_br_ml, w_out)
    if not need_ctx:
        return y_l, None
    hy_out_c = hyena_mixer(hy_c, hy_conv_w, hy_conv_b, hy_filter_params, hy_bias, ctx_grid)
    y_c = gated_merge(hy_out_c, s5_out_c, ml_out_c, mg_c, w_br_hy, w_br_s5, w_br_ml, w_out)
    return y_l, y_c


def moe(h, router_w, router_b, w1, b1, w2, b2):
    bsz, seq_len, d = h.shape
    t = h.reshape(-1, d)
    n_tok = t.shape[0]
    logits = (t @ router_w + router_b).astype(jnp.float32)
    top_val, top_idx = lax.top_k(logits, TOP_K)
    weights = jax.nn.softmax(top_val, axis=-1).astype(h.dtype)
    flat_e = top_idx.reshape(-1)
    n_assign = flat_e.shape[0]
    n_blocks = -(-n_assign // MOE_BLOCK) + N_EXPERTS
    n_slots = n_blocks * MOE_BLOCK
    order = jnp.argsort(flat_e, stable=True)
    e_sorted = flat_e[order]
    counts = jnp.bincount(flat_e, length=N_EXPERTS).astype(jnp.int32)
    padded = (counts + MOE_BLOCK - 1) // MOE_BLOCK * MOE_BLOCK
    end_pad = jnp.cumsum(padded)
    start_pad = end_pad - padded
    start = jnp.cumsum(counts) - counts
    dest = start_pad[e_sorted] + jnp.arange(n_assign, dtype=jnp.int32) - start[e_sorted]
    slot_tok = jnp.full((n_slots,), n_tok, jnp.int32).at[dest].set((order // TOP_K).astype(jnp.int32))
    slot_w = jnp.zeros((n_slots,), h.dtype).at[dest].set(weights.reshape(-1)[order])
    block_start = jnp.arange(n_blocks, dtype=jnp.int32) * MOE_BLOCK
    block_e = jnp.minimum(jnp.searchsorted(end_pad, block_start, side='right'), N_EXPERTS - 1).astype(jnp.int32)
    t_pad = jnp.concatenate([t, jnp.zeros((1, d), t.dtype)], axis=0)

    def block_ffn(blk):
        e, tok, wt = blk
        hid = t_pad[tok] @ w1[e] + b1[e]
        x_glu = jnp.minimum(hid[:, 0::2], SWIGLU_LIMIT)
        x_lin = jnp.clip(hid[:, 1::2], -SWIGLU_LIMIT, SWIGLU_LIMIT)
        act = x_glu * jax.nn.sigmoid(SWIGLU_ALPHA * x_glu) * (x_lin + 1)
        return (act @ w2[e] + b2[e]) * wt[:, None]

    out = lax.map(block_ffn, (block_e, slot_tok.reshape(n_blocks, MOE_BLOCK),
                              slot_w.reshape(n_blocks, MOE_BLOCK)))
    y = jnp.zeros((n_tok + 1, d), h.dtype).at[slot_tok].add(out.reshape(n_slots, d))
    return y[:n_tok].reshape(bsz, seq_len, d)


def setup_inputs(seed: int = 0) -> dict:
    key = jax.random.key(seed)
    ks = iter(jax.random.split(key, 64))
    nrm = lambda shape, scale: scale * jax.random.normal(next(ks), shape, jnp.float32)
    D, NL, H = D_MODEL, DEPTH, HY_FILTER_HIDDEN
    G, P, N = S5_GROUPS, S5_STATE, S5_GROUP
    i_part = nrm((NL, 2, 1, ML_HEADS), 0.1)
    f_part = jnp.linspace(3.0, 6.0, ML_HEADS, dtype=jnp.float32) + nrm((NL, 2, 1, ML_HEADS), 0.1)
    return {
        'x': nrm((BATCH, SEQ, D), 1.0),
        'c': nrm((BATCH, D), 1.0),
        'ctx': nrm((BATCH, CTX_LEN, D), 1.0),
        'c_ctx': nrm((D,), 1.0),
        'ada_w': nrm((NL, D, 6 * D), 0.5 * D ** -0.5),
        'ada_b': nrm((NL, 6 * D), 0.02),
        'norm1_g': 1.0 + nrm((NL, D), 0.02),
        'norm2_g': 1.0 + nrm((NL, D), 0.02),
        'final_norm_g': 1.0 + nrm((D,), 0.02),
        'w_in': nrm((NL, D, N_IN), D ** -0.5),
        'hy_conv_w': nrm((NL, SHORT_CONV, (1 + HY_ORDER) * D_HY), SHORT_CONV ** -0.5),
        'hy_conv_b': nrm((NL, (1 + HY_ORDER) * D_HY), 0.02),
        'hy_f_w1': nrm((NL, HY_EMB, H), HY_EMB ** -0.5),
        'hy_f_b1': nrm((NL, H), 0.02),
        'hy_f_w2': nrm((NL, H, H), H ** -0.5),
        'hy_f_b2': nrm((NL, H), 0.02),
        'hy_f_w3': nrm((NL, H, HY_ORDER * 2 * D_HY), H ** -0.5),
        'hy_f_b3': nrm((NL, HY_ORDER * 2 * D_HY), 0.02),
        'hy_sin_freq': 1.0 + nrm((NL, 2, H), 0.02),
        'hy_bias': nrm((NL, HY_ORDER, D_HY), 1.0),
        's5_lam_re': -0.5 + nrm((NL, 2, G, P), 0.01),
        's5_lam_im': math.pi * jnp.arange(P, dtype=jnp.float32) + nrm((NL, 2, G, P), 0.01),
        's5_log_dt': jax.random.uniform(next(ks), (NL, 2, G), jnp.float32,
                                        math.log(S5_DT_MIN), math.log(S5_DT_MAX)),
        's5_b_re': nrm((NL, G, P, N), (2 * N) ** -0.5),
        's5_b_im': nrm((NL, G, P, N), (2 * N) ** -0.5),
        's5_c_re': nrm((NL, G, N, P), P ** -0.5),
        's5_c_im': nrm((NL, G, N, P), P ** -0.5),
        's5_d': nrm((NL, D_S5), 0.5),
        's5_glu_w': nrm((NL, D_S5, D_S5), D_S5 ** -0.5),
        'ml_conv_w': nrm((NL, SHORT_CONV, 2 * D_ML), SHORT_CONV ** -0.5),
        'ml_conv_b': nrm((NL, 2 * D_ML), 0.02),
        'ml_gate_b': jnp.concatenate([i_part, f_part], axis=2).reshape(NL, 4 * ML_HEADS),
        'ml_norm_g': 1.0 + nrm((NL, D_ML), 0.02),
        'w_br_hy': nrm((NL, D_HY, D), D_HY ** -0.5),
        'w_br_s5': nrm((NL, D_S5, D), D_S5 ** -0.5),
        'w_br_ml': nrm((NL, D_ML, D), D_ML ** -0.5),
        'w_out': nrm((NL, D, D), D ** -0.5),
        'moe_router_w': nrm((NL, D, N_EXPERTS), D ** -0.5),
        'moe_router_b': nrm((NL, N_EXPERTS), 0.01),
        'moe_w1': nrm((NL, N_EXPERTS, D, 2 * D_FF_EXPERT), D ** -0.5),
        'moe_b1': nrm((NL, N_EXPERTS, 2 * D_FF_EXPERT), 0.02),
        'moe_w2': nrm((NL, N_EXPERTS, D_FF_EXPERT, D), D_FF_EXPERT ** -0.5),
        'moe_b2': nrm((NL, N_EXPERTS, D), 0.02),
    }


def reference(x, c, ctx, c_ctx, ada_w, ada_b, norm1_g, norm2_g, final_norm_g, w_in,
              hy_conv_w, hy_conv_b, hy_f_w1, hy_f_b1, hy_f_w2, hy_f_b2, hy_f_w3, hy_f_b3, hy_sin_freq, hy_bias,
              s5_lam_re, s5_lam_im, s5_log_dt, s5_b_re, s5_b_im, s5_c_re, s5_c_im, s5_d, s5_glu_w,
              ml_conv_w, ml_conv_b, ml_gate_b, ml_norm_g,
              w_br_hy, w_br_s5, w_br_ml, w_out,
              moe_router_w, moe_router_b, moe_w1, moe_b1, moe_w2, moe_b2):
    seq_len = x.shape[1]
    ctx_len = ctx.shape[1]
    rows = seq_len // GRID_W
    lat_grid = (rows, GRID_W)
    ctx_grid = (1, ctx_len)
    silu_c = jax.nn.silu(c.astype(jnp.float32))
    silu_cc = jax.nn.silu(c_ctx.astype(jnp.float32))[None]
    for l in range(DEPTH):
        need_ctx = l < DEPTH - 1
        mod_l = (silu_c @ ada_w[l] + ada_b[l]).astype(x.dtype)[:, None, :]
        mod_c = (silu_cc @ ada_w[l] + ada_b[l]).astype(ctx.dtype)[:, None, :]
        sh1_l, sc1_l, g1_l, sh2_l, sc2_l, g2_l = jnp.split(mod_l, 6, axis=-1)
        sh1_c, sc1_c, g1_c, sh2_c, sc2_c, g2_c = jnp.split(mod_c, 6, axis=-1)
        h_l = modulate(rmsnorm(x, norm1_g[l]), sh1_l, sc1_l)
        h_c = modulate(rmsnorm(ctx, norm1_g[l]), sh1_c, sc1_c)
        y_l, y_c = token_mixer(
            h_l, h_c, lat_grid, ctx_grid, need_ctx, w_in[l], hy_conv_w[l], hy_conv_b[l],
            (hy_f_w1[l], hy_f_b1[l], hy_f_w2[l], hy_f_b2[l], hy_f_w3[l], hy_f_b3[l], hy_sin_freq[l]), hy_bias[l],
            (s5_lam_re[l], s5_lam_im[l], s5_log_dt[l], s5_b_re[l], s5_b_im[l], s5_c_re[l], s5_c_im[l],
             s5_d[l], s5_glu_w[l]),
            ml_conv_w[l], ml_conv_b[l], ml_gate_b[l], ml_norm_g[l],
            w_br_hy[l], w_br_s5[l], w_br_ml[l], w_out[l])
        x = x + g1_l * y_l
        h2_l = modulate(rmsnorm(x, norm2_g[l]), sh2_l, sc2_l)
        moe_args = (moe_router_w[l], moe_router_b[l], moe_w1[l], moe_b1[l], moe_w2[l], moe_b2[l])
        if need_ctx:
            ctx = ctx + g1_c * y_c
            h2_c = modulate(rmsnorm(ctx, norm2_g[l]), sh2_c, sc2_c)
            f = moe(jnp.concatenate([h2_c, h2_l], axis=1), *moe_args)
            ctx = ctx + g2_c * f[:, :ctx_len]
            x = x + g2_l * f[:, ctx_len:]
        else:
            x = x + g2_l * moe(h2_l, *moe_args)
    return rmsnorm(x, final_norm_g)
```

```python
import functools
import math

import jax
import jax.numpy as jnp
import numpy as np
from jax import lax
from jax.experimental import pallas as pl
from jax.experimental.pallas import tpu as pltpu

D_MODEL = 1024
DEPTH = 4
GRID_W = 64

D_HY = 384
D_S5 = 384
D_ML = 384
N_BRANCH = 3
SHORT_CONV = 3

HY_ORDER = 2
HY_EMB = 33
HY_BANDS = (HY_EMB - 1) // 2
HY_DECAY_TARGET = 1e-2
HY_FAST_DECAY_PCT = 0.3
HY_SLOW_DECAY_PCT = 1.5
HY_MIN_DECAY = math.log(HY_DECAY_TARGET) / HY_SLOW_DECAY_PCT
HY_MAX_DECAY = math.log(HY_DECAY_TARGET) / HY_FAST_DECAY_PCT

S5_GROUP = 16
S5_GROUPS = D_S5 // S5_GROUP
S5_STATE = 64

ML_HEADS = 4
ML_HEAD_DIM = D_ML // ML_HEADS
ML_CHUNK = 64
NEG = -1e30

N_EXPERTS = 32
TOP_K = 4
D_FF_EXPERT = 512
SWIGLU_LIMIT = 7.0
SWIGLU_ALPHA = 1.702

NORM_EPS = 1e-6

COL_SIZES = (D_S5, 2 * D_ML, D_ML, 4 * ML_HEADS, (1 + HY_ORDER) * D_HY, D_ML, N_BRANCH * D_MODEL)

LANES = 128
VMEM_LIMIT_BYTES = 56 * 1024 * 1024

GATE_PAD = LANES
PROJ_SIZES = (D_S5, 2 * D_ML, D_ML, (1 + HY_ORDER) * D_HY, D_ML, N_BRANCH * D_MODEL, GATE_PAD)
PROJ_COLS = sum(PROJ_SIZES)
PROJ_TN = 896
PROJ_TM = 1024

MOE_ROWS = 512
MERGE_TM = 512


def _proj_kernel(x_ref, shift_ref, scale_ref, g_ref, w_ref, o_ref, h_scr):
    @pl.when(pl.program_id(1) == 0)
    def _():
        x = x_ref[...]
        ms = jnp.mean(x * x, axis=-1, keepdims=True)
        y = x * lax.rsqrt(ms + NORM_EPS) * g_ref[...]
        h = y * (1.0 + scale_ref[0]) + shift_ref[0]
        h_scr[...] = h.astype(jnp.bfloat16)

    o_ref[...] = jnp.dot(h_scr[...], w_ref[...], preferred_element_type=jnp.float32)


def norm_mod_project(x2d, shift, scale, gain, w_bf16, rows_per_mod, tm):
    rows, d = x2d.shape
    n = w_bf16.shape[1]
    blocks_per_mod = rows_per_mod // tm
    grid = (rows // tm, n // PROJ_TN)
    return pl.pallas_call(
        _proj_kernel,
        grid=grid,
        in_specs=[
            pl.BlockSpec((tm, d), lambda i, j: (i, 0)),
            pl.BlockSpec((1, 1, d), lambda i, j: (i // blocks_per_mod, 0, 0)),
            pl.BlockSpec((1, 1, d), lambda i, j: (i // blocks_per_mod, 0, 0)),
            pl.BlockSpec((1, d), lambda i, j: (0, 0)),
            pl.BlockSpec((d, PROJ_TN), lambda i, j: (0, j)),
        ],
        out_specs=pl.BlockSpec((tm, PROJ_TN), lambda i, j: (i, j)),
        out_shape=jax.ShapeDtypeStruct((rows, n), jnp.float32),
        scratch_shapes=[pltpu.VMEM((tm, d), jnp.bfloat16)],
        compiler_params=pltpu.CompilerParams(
            dimension_semantics=("parallel", "arbitrary"),
            vmem_limit_bytes=VMEM_LIMIT_BYTES),
        name="norm_mod_project",
    )(x2d, shift, scale, gain, w_bf16)


def _merge_kernel(hy_ref, s5_ref, ml_ref, mg_ref, x_ref, g1_ref, wh_ref, ws_ref, wm_ref, wo_ref, o_ref):
    bf = jnp.bfloat16
    d = D_MODEL
    y = jax.nn.sigmoid(mg_ref[:, 0:d]) * jnp.dot(hy_ref[...].astype(bf), wh_ref[...],
                                                 preferred_element_type=jnp.float32)
    y = y + jax.nn.sigmoid(mg_ref[:, d:2 * d]) * jnp.dot(s5_ref[...].astype(bf), ws_ref[...],
                                                         preferred_element_type=jnp.float32)
    y = y + jax.nn.sigmoid(mg_ref[:, 2 * d:3 * d]) * jnp.dot(ml_ref[...].astype(bf), wm_ref[...],
                                                             preferred_element_type=jnp.float32)
    out = jnp.dot(y.astype(bf), wo_ref[...], preferred_element_type=jnp.float32)
    o_ref[...] = x_ref[...] + g1_ref[0] * out


def merge_project_residual(hy, s5, ml, mg, x2d, g1, wh, ws, wm, wo, rows_per_mod, tm):
    rows, d = x2d.shape
    blocks_per_mod = rows_per_mod // tm
    row_spec = lambda c: pl.BlockSpec((tm, c), lambda i: (i, 0))
    full = lambda a: pl.BlockSpec(a.shape, lambda i: (0, 0))
    return pl.pallas_call(
        _merge_kernel,
        grid=(rows // tm,),
        in_specs=[row_spec(D_HY), row_spec(D_S5), row_spec(D_ML), row_spec(N_BRANCH * d), row_spec(d),
                  pl.BlockSpec((1, 1, d), lambda i: (i // blocks_per_mod, 0, 0)),
                  full(wh), full(ws), full(wm), full(wo)],
        out_specs=row_spec(d),
        out_shape=jax.ShapeDtypeStruct((rows, d), jnp.float32),
        compiler_params=pltpu.CompilerParams(
            dimension_semantics=("parallel",),
            vmem_limit_bytes=VMEM_LIMIT_BYTES),
        name="merge_project_residual",
    )(hy, s5, ml, mg, x2d, g1, wh, ws, wm, wo)


def _moe_kernel(be_ref, nu_ref, x_ref, wt_ref, w1g_ref, w1l_ref, b1g_ref, b1l_ref, w2_ref, b2_ref, o_ref):
    @pl.when(pl.program_id(0) < nu_ref[0])
    def _():
        x = x_ref[...]
        hg = jnp.dot(x, w1g_ref[0], preferred_element_type=jnp.float32) + b1g_ref[0]
        hl = jnp.dot(x, w1l_ref[0], preferred_element_type=jnp.float32) + b1l_ref[0]
        x_glu = jnp.minimum(hg, SWIGLU_LIMIT)
        x_lin = jnp.clip(hl, -SWIGLU_LIMIT, SWIGLU_LIMIT)
        act = x_glu * jax.nn.sigmoid(SWIGLU_ALPHA * x_glu) * (x_lin + 1.0)
        out = jnp.dot(act.astype(jnp.bfloat16), w2_ref[0], preferred_element_type=jnp.float32) + b2_ref[0]
        o_ref[...] = out * wt_ref[...]


def moe_expert_blocks(block_e, n_used, x_sorted, slot_w, w1g, w1l, b1g, b1l, w2, b2):
    n_slots, d = x_sorted.shape
    n_blocks = n_slots // MOE_ROWS
    f = D_FF_EXPERT

    def row_map(i, be, nu):
        return (jnp.minimum(i, nu[0] - 1), 0)

    def w_map(i, be, nu):
        return (be[jnp.minimum(i, nu[0] - 1)], 0, 0)

    grid_spec = pltpu.PrefetchScalarGridSpec(
        num_scalar_prefetch=2,
        grid=(n_blocks,),
        in_specs=[
            pl.BlockSpec((MOE_ROWS, d), row_map),
            pl.BlockSpec((MOE_ROWS, 1), row_map),
            pl.BlockSpec((1, d, f), w_map),
            pl.BlockSpec((1, d, f), w_map),
            pl.BlockSpec((1, 1, f), w_map),
            pl.BlockSpec((1, 1, f), w_map),
            pl.BlockSpec((1, f, d), w_map),
            pl.BlockSpec((1, 1, d), w_map),
        ],
        out_specs=pl.BlockSpec((MOE_ROWS, d), row_map),
    )
    return pl.pallas_call(
        _moe_kernel,
        grid_spec=grid_spec,
        out_shape=jax.ShapeDtypeStruct((n_slots, d), jnp.float32),
        compiler_params=pltpu.CompilerParams(
            dimension_semantics=("arbitrary",),
            vmem_limit_bytes=VMEM_LIMIT_BYTES),
        name="moe_expert_blocks",
    )(block_e, n_used, x_sorted, slot_w, w1g, w1l, b1g, b1l, w2, b2)


def rmsnorm(x, g):
    xf = x.astype(jnp.float32)
    y = xf * lax.rsqrt(jnp.mean(xf * xf, axis=-1, keepdims=True) + NORM_EPS)
    return (y * g.astype(jnp.float32)).astype(x.dtype)


def modulate(h, shift, scale):
    return h * (1 + scale) + shift


def short_conv(u, w, b, grid):
    bsz, _, ch = u.shape
    n_rows, row_len = grid
    pad = SHORT_CONV // 2
    r = jnp.pad(u.reshape(bsz, n_rows, row_len, ch), ((0, 0), (0, 0), (pad, pad), (0, 0)))
    y = b + r[:, :, 0:row_len] * w[0]
    for j in range(1, SHORT_CONV):
        y = y + r[:, :, j:j + row_len] * w[j]
    return y.reshape(bsz, n_rows * row_len, ch)


def hyena_filters(seq_len, w1, b1, w2, b2, w3, b3, sin_freq):
    t = jnp.linspace(0.0, 1.0, seq_len, dtype=jnp.float32)[:, None]
    w = 2.0 * math.pi * jnp.arange(seq_len, dtype=jnp.float32)[:, None] / seq_len
    f = jnp.linspace(1e-4, HY_BANDS - 1, HY_BANDS, dtype=jnp.float32)[None, :]
    z = jnp.concatenate([t, jnp.cos(f * w), -jnp.sin(f * w)], axis=-1)
    hdn = jnp.sin(sin_freq[0] * (z @ w1 + b1))
    hdn = jnp.sin(sin_freq[1] * (hdn @ w2 + b2))
    hf = (hdn @ w3 + b3).astype(jnp.float32).reshape(seq_len, HY_ORDER, 2, D_HY)
    deltas = jnp.abs(jnp.linspace(HY_MIN_DECAY, HY_MAX_DECAY, D_HY, dtype=jnp.float32))
    decay = jnp.exp(-t * deltas)
    return hf * decay[:, None, None, :]


def bidir_fftconv(u, h_fwd, h_bwd, skip):
    seq_len = u.shape[1]
    n_fft = 2 * seq_len
    k = jnp.concatenate([h_fwd, jnp.zeros_like(h_fwd[:1]), h_bwd[:0:-1]], axis=0)
    k = k / jnp.sum(jnp.abs(k), axis=0, keepdims=True)
    uf = u.astype(jnp.float32)
    spec = jnp.fft.rfft(uf, n=n_fft, axis=1) * jnp.fft.rfft(k, n=n_fft, axis=0)[None]
    y = jnp.fft.irfft(spec, n=n_fft, axis=1)[:, :seq_len]
    return (y + uf * skip.astype(jnp.float32)).astype(u.dtype)


def hyena_mixer(u, conv_w, conv_b, filter_params, bias, grid):
    filt = hyena_filters(u.shape[1], *filter_params)
    parts = jnp.split(short_conv(u, conv_w, conv_b, grid), 1 + HY_ORDER, axis=-1)
    z = parts[0]
    for o in range(HY_ORDER):
        z = parts[1 + o] * bidir_fftconv(z, filt[:, o, 0], filt[:, o, 1], bias[o])
    return z


def s5_discretize(lam_re, lam_im, log_dt, b_mat):
    lam = lax.complex(lam_re.astype(jnp.float32), lam_im.astype(jnp.float32))
    dt = jnp.exp(log_dt.astype(jnp.float32))[:, None]
    lam_bar = jnp.exp(lam * dt)
    b_bar = ((lam_bar - 1.0) / lam)[..., None] * b_mat
    return lam_bar, b_bar


def linear_scan(bu, lam_bar, x0):
    bu = bu.at[:, 0].add(lam_bar * x0)
    a = jnp.broadcast_to(lam_bar, (1, bu.shape[1]) + lam_bar.shape)

    def combine(left, right):
        return (left[0] * right[0], right[0] * left[1] + right[1])

    _, xs = lax.associative_scan(combine, (a, bu), axis=1)
    return xs


def s5_mixer(u_l, u_c, lam_re, lam_im, log_dt, b_re, b_im, c_re, c_im, d_skip, glu_w, need_ctx):
    bsz, seq_len, _ = u_l.shape
    ctx_len = u_c.shape[1]
    ul = u_l.astype(jnp.float32).reshape(bsz, seq_len, S5_GROUPS, S5_GROUP)
    uc = u_c.astype(jnp.float32).reshape(bsz, ctx_len, S5_GROUPS, S5_GROUP)
    b_mat = lax.complex(b_re.astype(jnp.float32), b_im.astype(jnp.float32))
    c_mat = lax.complex(c_re.astype(jnp.float32), c_im.astype(jnp.float32))
    dsk = d_skip.astype(jnp.float32).reshape(S5_GROUPS, S5_GROUP)
    x0 = jnp.zeros((bsz, S5_GROUPS, S5_STATE), jnp.complex64)
    y_l = dsk * ul
    y_c = dsk * uc if need_ctx else None
    for d in range(2):
        flip = (lambda t: jnp.flip(t, axis=1)) if d == 1 else (lambda t: t)
        lam_bar, b_bar = s5_discretize(lam_re[d], lam_im[d], log_dt[d], b_mat)
        xs_c = linear_scan(flip(jnp.einsum('blgn,gpn->blgp', uc.astype(jnp.complex64), b_bar)), lam_bar, x0)
        xs_l = flip(linear_scan(flip(jnp.einsum('blgn,gpn->blgp', ul.astype(jnp.complex64), b_bar)),
                                lam_bar, xs_c[:, -1]))
        y_l = y_l + jnp.real(jnp.einsum('blgp,gnp->blgn', xs_l, c_mat))
        if need_ctx:
            y_c = y_c + jnp.real(jnp.einsum('blgp,gnp->blgn', flip(xs_c), c_mat))

    def half_glu(y, dtype):
        g = jax.nn.gelu(y.reshape(bsz, -1, D_S5))
        return (g * jax.nn.sigmoid(g @ glu_w)).astype(dtype)

    return half_glu(y_l, u_l.dtype), (half_glu(y_c, u_c.dtype) if need_ctx else None)


def mlstm_chunkwise(q, k, v, log_i, log_f, state0, with_output):
    bsz, nh, seq_len, dh = q.shape
    nc = seq_len // ML_CHUNK
    blk = lambda t: t.reshape(bsz, nh, nc, ML_CHUNK, *t.shape[3:])
    qc, kc, vc, ic, fc = blk(q), blk(k), blk(v), blk(log_i), blk(log_f)
    b = jnp.cumsum(fc, axis=-1)
    b_end = b[..., -1]
    g = b_end[..., None] - b + ic

    def chunk_step(carry, inp):
        c_mem, n_mem, m_mem = carry
        k_t, v_t, g_t, be = inp
        m_new = jnp.maximum(be + m_mem, jnp.max(g_t, axis=-1))
        decay = jnp.exp(be + m_mem - m_new)
        w = jnp.exp(g_t - m_new[..., None])
        c_new = decay[..., None, None] * c_mem + jnp.einsum('bhs,bhsd,bhse->bhde', w, v_t, k_t)
        n_new = decay[..., None] * n_mem + jnp.einsum('bhs,bhse->bhe', w, k_t)
        return (c_new, n_new, m_new), (c_mem, n_mem, m_mem)

    to_scan = lambda t: jnp.moveaxis(t, 2, 0)
    final, (c_prev, n_prev, m_prev) = lax.scan(
        chunk_step, state0, (to_scan(kc), to_scan(vc), to_scan(g), to_scan(b_end)))
    if not with_output:
        return None, final
    c_prev = jnp.moveaxis(c_prev, 0, 2)
    n_prev = jnp.moveaxis(n_prev, 0, 2)
    m_prev = jnp.moveaxis(m_prev, 0, 2)
    lower = jnp.tril(jnp.ones((ML_CHUNK, ML_CHUNK), dtype=bool))
    a = b + m_prev[..., None]
    dmat = jnp.where(lower, b[..., :, None] - b[..., None, :] + ic[..., None, :], NEG)
    m_out = jnp.maximum(a, jnp.max(dmat, axis=-1))
    wmat = jnp.exp(dmat - m_out[..., None])
    wa = jnp.exp(a - m_out)
    s = jnp.einsum('bhctd,bhcsd->bhcts', qc, kc) * wmat
    num = jnp.einsum('bhcts,bhcsd->bhctd', s, vc) + wa[..., None] * jnp.einsum('bhcde,bhcte->bhctd', c_prev, qc)
    den = jnp.sum(s, axis=-1) + wa * jnp.einsum('bhce,bhcte->bhct', n_prev, qc)
    h = num / jnp.maximum(jnp.abs(den), jnp.exp(-m_out))[..., None]
    return h.reshape(bsz, nh, seq_len, dh), final


def mlstm_heads(qk, v, gates, gate_b, conv_w, conv_b, grid):
    bsz, seq_len, _ = v.shape
    qk = jax.nn.silu(short_conv(qk, conv_w, conv_b, grid)).astype(jnp.float32)
    q, k = jnp.split(qk, 2, axis=-1)
    to_heads = lambda t: t.reshape(bsz, seq_len, ML_HEADS, ML_HEAD_DIM).transpose(0, 2, 1, 3)
    g = (gates + gate_b).astype(jnp.float32).reshape(bsz, seq_len, 4, ML_HEADS).transpose(2, 0, 3, 1)
    return to_heads(q), to_heads(k) * ML_HEAD_DIM ** -0.5, to_heads(v.astype(jnp.float32)), g


def mlstm_head_out(h, o, norm_g):
    bsz, _, seq_len, _ = h.shape
    h = h.transpose(0, 2, 1, 3)
    h = h * lax.rsqrt(jnp.mean(h * h, axis=-1, keepdims=True) + NORM_EPS)
    h = h.reshape(bsz, seq_len, D_ML) * norm_g.astype(jnp.float32)
    return (h * jax.nn.sigmoid(o.astype(jnp.float32))).astype(o.dtype)


def mlstm_mixer(qk_l, v_l, gt_l, o_l, qk_c, v_c, gt_c, o_c, gate_b, conv_w, conv_b, norm_g,
                lat_grid, ctx_grid, need_ctx):
    ql, kl, vl, gl = mlstm_heads(qk_l, v_l, gt_l, gate_b, conv_w, conv_b, lat_grid)
    qc, kc, vc, gc = mlstm_heads(qk_c, v_c, gt_c, gate_b, conv_w, conv_b, ctx_grid)
    bsz = ql.shape[0]
    state0 = (jnp.zeros((bsz, ML_HEADS, ML_HEAD_DIM, ML_HEAD_DIM), jnp.float32),
              jnp.zeros((bsz, ML_HEADS, ML_HEAD_DIM), jnp.float32),
              jnp.full((bsz, ML_HEADS), NEG, jnp.float32))
    h_l = jnp.zeros_like(ql)
    h_c = jnp.zeros_like(qc) if need_ctx else None
    for d in range(2):
        flip = (lambda t: jnp.flip(t, axis=2)) if d == 1 else (lambda t: t)
        hc_d, state_c = mlstm_chunkwise(flip(qc), flip(kc), flip(vc), flip(gc[2 * d]),
                                        flip(jax.nn.log_sigmoid(gc[2 * d + 1])), state0, need_ctx)
        hl_d, _ = mlstm_chunkwise(flip(ql), flip(kl), flip(vl), flip(gl[2 * d]),
                                  flip(jax.nn.log_sigmoid(gl[2 * d + 1])), state_c, True)
        h_l = h_l + flip(hl_d)
        if need_ctx:
            h_c = h_c + flip(hc_d)
    out_l = mlstm_head_out(h_l, o_l, norm_g)
    out_c = mlstm_head_out(h_c, o_c, norm_g) if need_ctx else None
    return out_l, out_c


def split_proj(p):
    out, start = [], 0
    for s in PROJ_SIZES:
        out.append(p[..., start:start + s])
        start += s
    s5, qk, v, hy, o, mg, gt = out
    return s5, qk, v, gt[..., :4 * ML_HEADS], hy, o, mg


def permute_w_in(w_in):
    cols, start = [], 0
    for s in COL_SIZES:
        cols.append(w_in[:, start:start + s])
        start += s
    s5, qk, v, gt, hy, o, mg = cols
    gt = jnp.pad(gt, ((0, 0), (0, GATE_PAD - gt.shape[1])))
    return jnp.concatenate([s5, qk, v, hy, o, mg, gt], axis=1).astype(jnp.bfloat16)


def moe(t, router_w, router_b, w1g, w1l, b1g, b1l, w2, b2):
    n_tok, d = t.shape
    logits = (t @ router_w + router_b).astype(jnp.float32)
    top_val, top_idx = lax.top_k(logits, TOP_K)
    weights = jax.nn.softmax(top_val, axis=-1).astype(t.dtype)
    flat_e = top_idx.reshape(-1)
    n_assign = flat_e.shape[0]
    n_blocks = -(-n_assign // MOE_ROWS) + N_EXPERTS
    n_slots = n_blocks * MOE_ROWS
    order = jnp.argsort(flat_e, stable=True)
    e_sorted = flat_e[order]
    counts = jnp.bincount(flat_e, length=N_EXPERTS).astype(jnp.int32)
    padded = (counts + MOE_ROWS - 1) // MOE_ROWS * MOE_ROWS
    end_pad = jnp.cumsum(padded)
    start_pad = end_pad - padded
    start = jnp.cumsum(counts) - counts
    dest = start_pad[e_sorted] + jnp.arange(n_assign, dtype=jnp.int32) - start[e_sorted]
    slot_tok = jnp.zeros((n_slots,), jnp.int32).at[dest].set((order // TOP_K).astype(jnp.int32))
    slot_w = jnp.zeros((n_slots,), t.dtype).at[dest].set(weights.reshape(-1)[order])
    block_start = jnp.arange(n_blocks, dtype=jnp.int32) * MOE_ROWS
    block_e = jnp.minimum(jnp.searchsorted(end_pad, block_start, side='right'), N_EXPERTS - 1).astype(jnp.int32)
    n_used = (end_pad[-1] // MOE_ROWS).astype(jnp.int32).reshape(1)
    pos = jnp.zeros((n_assign,), jnp.int32).at[order].set(dest)
    x_sorted = t.astype(jnp.bfloat16)[slot_tok]
    out = moe_expert_blocks(block_e, n_used, x_sorted, slot_w.reshape(n_slots, 1),
                            w1g, w1l, b1g, b1l, w2, b2)
    return out[pos].reshape(n_tok, TOP_K, d).sum(axis=1)


def kernel(x, c, ctx, c_ctx, ada_w, ada_b, norm1_g, norm2_g, final_norm_g, w_in,
           hy_conv_w, hy_conv_b, hy_f_w1, hy_f_b1, hy_f_w2, hy_f_b2, hy_f_w3, hy_f_b3, hy_sin_freq, hy_bias,
           s5_lam_re, s5_lam_im, s5_log_dt, s5_b_re, s5_b_im, s5_c_re, s5_c_im, s5_d, s5_glu_w,
           ml_conv_w, ml_conv_b, ml_gate_b, ml_norm_g,
           w_br_hy, w_br_s5, w_br_ml, w_out,
           moe_router_w, moe_router_b, moe_w1, moe_b1, moe_w2, moe_b2):
    bsz, seq_len, d = x.shape
    ctx_len = ctx.shape[1]
    rows = seq_len // GRID_W
    lat_grid = (rows, GRID_W)
    ctx_grid = (1, ctx_len)
    bf = jnp.bfloat16
    silu_c = jax.nn.silu(c.astype(jnp.float32))
    silu_cc = jax.nn.silu(c_ctx.astype(jnp.float32))[None]
    for l in range(DEPTH):
        need_ctx = l < DEPTH - 1
        mod_l = (silu_c @ ada_w[l] + ada_b[l]).astype(x.dtype)[:, None, :]
        mod_c = (silu_cc @ ada_w[l] + ada_b[l]).astype(ctx.dtype)[:, None, :]
        sh1_l, sc1_l, g1_l, sh2_l, sc2_l, g2_l = jnp.split(mod_l, 6, axis=-1)
        sh1_c, sc1_c, g1_c, sh2_c, sc2_c, g2_c = jnp.split(mod_c, 6, axis=-1)

        w_in_p = permute_w_in(w_in[l])
        gain1 = norm1_g[l].reshape(1, d)
        p_l = norm_mod_project(x.reshape(bsz * seq_len, d), sh1_l, sc1_l, gain1, w_in_p, seq_len, PROJ_TM)
        p_c = norm_mod_project(ctx.reshape(bsz * ctx_len, d), sh1_c, sc1_c, gain1, w_in_p,
                               bsz * ctx_len, PROJ_TM)
        s5_l, qk_l, v_l, gt_l, hy_l, o_l, mg_l = split_proj(p_l.reshape(bsz, seq_len, PROJ_COLS))
        s5_c, qk_c, v_c, gt_c, hy_c, o_c, mg_c = split_proj(p_c.reshape(bsz, ctx_len, PROJ_COLS))

        hy_params = (hy_f_w1[l], hy_f_b1[l], hy_f_w2[l], hy_f_b2[l], hy_f_w3[l], hy_f_b3[l], hy_sin_freq[l])
        hy_out_l = hyena_mixer(hy_l, hy_conv_w[l], hy_conv_b[l], hy_params, hy_bias[l], lat_grid)
        s5_out_l, s5_out_c = s5_mixer(s5_l, s5_c, s5_lam_re[l], s5_lam_im[l], s5_log_dt[l], s5_b_re[l],
                                      s5_b_im[l], s5_c_re[l], s5_c_im[l], s5_d[l], s5_glu_w[l], need_ctx)
        ml_out_l, ml_out_c = mlstm_mixer(qk_l, v_l, gt_l, o_l, qk_c, v_c, gt_c, o_c, ml_gate_b[l],
                                         ml_conv_w[l], ml_conv_b[l], ml_norm_g[l], lat_grid, ctx_grid, need_ctx)

        wh, ws, wm, wo = w_br_hy[l].astype(bf), w_br_s5[l].astype(bf), w_br_ml[l].astype(bf), w_out[l].astype(bf)
        flat = lambda a: a.reshape(-1, a.shape[-1])
        x = merge_project_residual(flat(hy_out_l), flat(s5_out_l), flat(ml_out_l), flat(mg_l), flat(x), g1_l,
                                   wh, ws, wm, wo, seq_len, MERGE_TM).reshape(bsz, seq_len, d)
        h2_l = modulate(rmsnorm(x, norm2_g[l]), sh2_l, sc2_l)

        w1 = moe_w1[l]
        moe_args = (moe_router_w[l], moe_router_b[l],
                    w1[:, :, 0::2].astype(bf), w1[:, :, 1::2].astype(bf),
                    moe_b1[l][:, None, 0::2], moe_b1[l][:, None, 1::2],
                    moe_w2[l].astype(bf), moe_b2[l][:, None, :])
        if need_ctx:
            hy_out_c = hyena_mixer(hy_c, hy_conv_w[l], hy_conv_b[l], hy_params, hy_bias[l], ctx_grid)
            ctx = merge_project_residual(flat(hy_out_c), flat(s5_out_c), flat(ml_out_c), flat(mg_c), flat(ctx),
                                         g1_c, wh, ws, wm, wo, bsz * ctx_len, MERGE_TM).reshape(bsz, ctx_len, d)
            h2_c = modulate(rmsnorm(ctx, norm2_g[l]), sh2_c, sc2_c)
            tok = jnp.concatenate([h2_c, h2_l], axis=1).reshape(-1, d)
            f = moe(tok, *moe_args).reshape(bsz, ctx_len + seq_len, d)
            ctx = ctx + g2_c * f[:, :ctx_len]
            x = x + g2_l * f[:, ctx_len:]
        else:
            x = x + g2_l * moe(h2_l.reshape(-1, d), *moe_args).reshape(bsz, seq_len, d)
    return rmsnorm(x, final_norm_g)
```

```python
import functools
import math

import jax
import jax.numpy as jnp
import numpy as np
from jax import lax
from jax.experimental import pallas as pl
from jax.experimental.pallas import tpu as pltpu

D_MODEL = 1024
DEPTH = 4
GRID_W = 64

D_HY = 384
D_S5 = 384
D_ML = 384
N_BRANCH = 3
SHORT_CONV = 3

HY_ORDER = 2
HY_EMB = 33
HY_BANDS = (HY_EMB - 1) // 2
HY_DECAY_TARGET = 1e-2
HY_FAST_DECAY_PCT = 0.3
HY_SLOW_DECAY_PCT = 1.5
HY_MIN_DECAY = math.log(HY_DECAY_TARGET) / HY_SLOW_DECAY_PCT
HY_MAX_DECAY = math.log(HY_DECAY_TARGET) / HY_FAST_DECAY_PCT

S5_GROUP = 16
S5_GROUPS = D_S5 // S5_GROUP
S5_STATE = 64

ML_HEADS = 4
ML_HEAD_DIM = D_ML // ML_HEADS
ML_CHUNK = 64
NEG = -1e30

N_EXPERTS = 32
TOP_K = 4
D_FF_EXPERT = 512
SWIGLU_LIMIT = 7.0
SWIGLU_ALPHA = 1.702

NORM_EPS = 1e-6

COL_SIZES = (D_S5, 2 * D_ML, D_ML, 4 * ML_HEADS, (1 + HY_ORDER) * D_HY, D_ML, N_BRANCH * D_MODEL)

LANES = 128
VMEM_LIMIT_BYTES = 56 * 1024 * 1024

GATE_PAD = LANES
PROJ_SIZES = (D_S5, 2 * D_ML, D_ML, (1 + HY_ORDER) * D_HY, D_ML, N_BRANCH * D_MODEL, GATE_PAD)
PROJ_COLS = sum(PROJ_SIZES)
PROJ_TN = 896
PROJ_TM = 1024

MOE_ROWS = 512
MERGE_TM = 512


def _proj_kernel(x_ref, shift_ref, scale_ref, g_ref, w_ref, o_ref, h_scr):
    @pl.when(pl.program_id(1) == 0)
    def _():
        x = x_ref[...]
        ms = jnp.mean(x * x, axis=-1, keepdims=True)
        y = x * lax.rsqrt(ms + NORM_EPS) * g_ref[...]
        h = y * (1.0 + scale_ref[0]) + shift_ref[0]
        h_scr[...] = h.astype(jnp.bfloat16)

    o_ref[...] = jnp.dot(h_scr[...], w_ref[...], preferred_element_type=jnp.float32)


def norm_mod_project(x2d, shift, scale, gain, w_bf16, rows_per_mod, tm):
    rows, d = x2d.shape
    n = w_bf16.shape[1]
    blocks_per_mod = rows_per_mod // tm
    grid = (rows // tm, n // PROJ_TN)
    return pl.pallas_call(
        _proj_kernel,
        grid=grid,
        in_specs=[
            pl.BlockSpec((tm, d), lambda i, j: (i, 0)),
            pl.BlockSpec((1, 1, d), lambda i, j: (i // blocks_per_mod, 0, 0)),
            pl.BlockSpec((1, 1, d), lambda i, j: (i // blocks_per_mod, 0, 0)),
            pl.BlockSpec((1, d), lambda i, j: (0, 0)),
            pl.BlockSpec((d, PROJ_TN), lambda i, j: (0, j)),
        ],
        out_specs=pl.BlockSpec((tm, PROJ_TN), lambda i, j: (i, j)),
        out_shape=jax.ShapeDtypeStruct((rows, n), jnp.float32),
        scratch_shapes=[pltpu.VMEM((tm, d), jnp.bfloat16)],
        compiler_params=pltpu.CompilerParams(
            dimension_semantics=("parallel", "arbitrary"),
            vmem_limit_bytes=VMEM_LIMIT_BYTES),
        name="norm_mod_project",
    )(x2d, shift, scale, gain, w_bf16)


def _merge_kernel(hy_ref, s5_ref, ml_ref, mg_ref, x_ref, g1_ref, wglu_ref, wh_ref, ws_ref, wm_ref, wo_ref, o_ref):
    bf = jnp.bfloat16
    d = D_MODEL
    g = jax.nn.gelu(s5_ref[...])
    s5 = g * jax.nn.sigmoid(jnp.dot(g.astype(bf), wglu_ref[...], preferred_element_type=jnp.float32))
    y = jax.nn.sigmoid(mg_ref[:, 0:d]) * jnp.dot(hy_ref[...].astype(bf), wh_ref[...],
                                                 preferred_element_type=jnp.float32)
    y = y + jax.nn.sigmoid(mg_ref[:, d:2 * d]) * jnp.dot(s5.astype(bf), ws_ref[...],
                                                         preferred_element_type=jnp.float32)
    y = y + jax.nn.sigmoid(mg_ref[:, 2 * d:3 * d]) * jnp.dot(ml_ref[...].astype(bf), wm_ref[...],
                                                             preferred_element_type=jnp.float32)
    out = jnp.dot(y.astype(bf), wo_ref[...], preferred_element_type=jnp.float32)
    o_ref[...] = x_ref[...] + g1_ref[0] * out


def merge_project_residual(hy, s5, ml, mg, x2d, g1, wglu, wh, ws, wm, wo, rows_per_mod, tm):
    rows, d = x2d.shape
    blocks_per_mod = rows_per_mod // tm
    row_spec = lambda c: pl.BlockSpec((tm, c), lambda i: (i, 0))
    full = lambda a: pl.BlockSpec(a.shape, lambda i: (0, 0))
    return pl.pallas_call(
        _merge_kernel,
        grid=(rows // tm,),
        in_specs=[row_spec(D_HY), row_spec(D_S5), row_spec(D_ML), row_spec(N_BRANCH * d), row_spec(d),
                  pl.BlockSpec((1, 1, d), lambda i: (i // blocks_per_mod, 0, 0)),
                  full(wglu), full(wh), full(ws), full(wm), full(wo)],
        out_specs=row_spec(d),
        out_shape=jax.ShapeDtypeStruct((rows, d), jnp.float32),
        compiler_params=pltpu.CompilerParams(
            dimension_semantics=("parallel",),
            vmem_limit_bytes=VMEM_LIMIT_BYTES),
        name="merge_project_residual",
    )(hy, s5, ml, mg, x2d, g1, wglu, wh, ws, wm, wo)


def _moe_kernel(be_ref, nu_ref, x_ref, wt_ref, w1g_ref, w1l_ref, b1g_ref, b1l_ref, w2_ref, b2_ref, o_ref):
    @pl.when(pl.program_id(0) < nu_ref[0])
    def _():
        x = x_ref[...]
        hg = jnp.dot(x, w1g_ref[0], preferred_element_type=jnp.float32) + b1g_ref[0]
        hl = jnp.dot(x, w1l_ref[0], preferred_element_type=jnp.float32) + b1l_ref[0]
        x_glu = jnp.minimum(hg, SWIGLU_LIMIT)
        x_lin = jnp.clip(hl, -SWIGLU_LIMIT, SWIGLU_LIMIT)
        act = x_glu * jax.nn.sigmoid(SWIGLU_ALPHA * x_glu) * (x_lin + 1.0)
        out = jnp.dot(act.astype(jnp.bfloat16), w2_ref[0], preferred_element_type=jnp.float32) + b2_ref[0]
        o_ref[...] = out * wt_ref[...]


def moe_expert_blocks(block_e, n_used, x_sorted, slot_w, w1g, w1l, b1g, b1l, w2, b2):
    n_slots, d = x_sorted.shape
    n_blocks = n_slots // MOE_ROWS
    f = D_FF_EXPERT

    def row_map(i, be, nu):
        return (jnp.minimum(i, nu[0] - 1), 0)

    def w_map(i, be, nu):
        return (be[jnp.minimum(i, nu[0] - 1)], 0, 0)

    grid_spec = pltpu.PrefetchScalarGridSpec(
        num_scalar_prefetch=2,
        grid=(n_blocks,),
        in_specs=[
            pl.BlockSpec((MOE_ROWS, d), row_map),
            pl.BlockSpec((MOE_ROWS, 1), row_map),
            pl.BlockSpec((1, d, f), w_map),
            pl.BlockSpec((1, d, f), w_map),
            pl.BlockSpec((1, 1, f), w_map),
            pl.BlockSpec((1, 1, f), w_map),
            pl.BlockSpec((1, f, d), w_map),
            pl.BlockSpec((1, 1, d), w_map),
        ],
        out_specs=pl.BlockSpec((MOE_ROWS, d), row_map),
    )
    return pl.pallas_call(
        _moe_kernel,
        grid_spec=grid_spec,
        out_shape=jax.ShapeDtypeStruct((n_slots, d), jnp.float32),
        compiler_params=pltpu.CompilerParams(
            dimension_semantics=("arbitrary",),
            vmem_limit_bytes=VMEM_LIMIT_BYTES),
        name="moe_expert_blocks",
    )(block_e, n_used, x_sorted, slot_w, w1g, w1l, b1g, b1l, w2, b2)


S5_T = 16
S5_W = S5_T * S5_GROUP
S5_HALF = 2 * S5_STATE


def _s5_kernel(u_ref, wp_ref, wm_ref, wq0_ref, wq1_ref, ar_ref, ai_ref, y_ref, s_scr, x_scr, yb_scr,
               *, n_batch, ctx_chunks, lat_chunks):
    nb = n_batch
    u = u_ref[0]
    s_scr[...] = jnp.dot(u, wp_ref[0], preferred_element_type=jnp.float32)
    ar = jnp.broadcast_to(ar_ref[0], (nb, S5_HALF))
    ai = jnp.broadcast_to(ai_ref[0], (nb, S5_HALF))

    def step(i, carry):
        xr, xi = carry
        r = pl.multiple_of(i * nb, nb)
        x_scr[pl.ds(r, nb), 0:S5_HALF] = xr
        x_scr[pl.ds(r, nb), S5_HALF:2 * S5_HALF] = xi
        sr = s_scr[pl.ds(r, nb), 0:S5_HALF]
        si = s_scr[pl.ds(r, nb), S5_HALF:2 * S5_HALF]
        return ar * xr - ai * xi + sr, ar * xi + ai * xr + si

    zero = jnp.zeros((nb, S5_HALF), jnp.float32)
    lax.fori_loop(0, ctx_chunks + lat_chunks, step, (zero, zero))

    xin = x_scr[...].astype(jnp.bfloat16)
    y_ref[0] = (jnp.dot(u[:, 0:S5_W], wm_ref[0], preferred_element_type=jnp.float32)
                + jnp.dot(xin, wq0_ref[0], preferred_element_type=jnp.float32))
    yb_scr[...] = jnp.dot(xin, wq1_ref[0], preferred_element_type=jnp.float32)
    for seg_start, seg_chunks in ((0, ctx_chunks), (ctx_chunks * nb, lat_chunks)):
        for i in range(seg_chunks):
            dst = seg_start + nb * i
            src = seg_start + nb * (seg_chunks - 1 - i)
            y_ref[0, dst:dst + nb, :] += yb_scr[src:src + nb, :]


def s5_scan_call(ucat, wp, wm, wq0, wq1, ar, ai, *, n_batch, ctx_chunks):
    groups, rows, _ = ucat.shape
    lat_chunks = rows // n_batch - ctx_chunks
    per_group = lambda a: pl.BlockSpec((1,) + a.shape[1:], lambda g: (g, 0, 0))
    return pl.pallas_call(
        functools.partial(_s5_kernel, n_batch=n_batch, ctx_chunks=ctx_chunks, lat_chunks=lat_chunks),
        grid=(groups,),
        in_specs=[per_group(a) for a in (ucat, wp, wm, wq0, wq1, ar, ai)],
        out_specs=pl.BlockSpec((1, rows, S5_W), lambda g: (g, 0, 0)),
        out_shape=jax.ShapeDtypeStruct((groups, rows, S5_W), jnp.float32),
        scratch_shapes=[pltpu.VMEM((rows, S5_W), jnp.float32)] * 3,
        compiler_params=pltpu.CompilerParams(
            dimension_semantics=("parallel",),
            vmem_limit_bytes=VMEM_LIMIT_BYTES),
        name="s5_scan",
    )(ucat, wp, wm, wq0, wq1, ar, ai)


def s5_chunk_weights(lam_re, lam_im, log_dt, b_re, b_im, c_re, c_im, d_skip):
    f32 = jnp.float32
    t_len, g_n, p_n, n_n = S5_T, S5_GROUPS, S5_STATE, S5_GROUP
    b_mat = lax.complex(b_re.astype(f32), b_im.astype(f32))
    c_mat = lax.complex(c_re.astype(f32), c_im.astype(f32))
    lam = lax.complex(lam_re.astype(f32), lam_im.astype(f32))
    lam_dt = lam * jnp.exp(log_dt.astype(f32))[..., None]
    b_bar = ((jnp.exp(lam_dt) - 1.0) / lam)[..., None] * b_mat
    j = jnp.arange(t_len + 1, dtype=f32)
    pw = jnp.exp(j[:, None, None, None] * lam_dt[None])

    kern = [jnp.real(jnp.einsum('gnp,jgp,gpm->jgnm', c_mat, pw[:t_len, d], b_bar[d])) for d in range(2)]
    idx = jnp.arange(t_len)
    diff = idx[None, :] - idx[:, None]
    k0 = kern[0][jnp.clip(diff, 0, None)]
    k1 = kern[1][jnp.clip(-diff, 0, None)]
    m5 = (jnp.where((diff >= 0)[:, :, None, None, None], k0, 0.0)
          + jnp.where((diff <= 0)[:, :, None, None, None], k1, 0.0))
    skip = jnp.eye(t_len, dtype=f32)[:, :, None, None, None] * (
        d_skip.astype(f32).reshape(g_n, n_n)[None, None, :, :, None] * jnp.eye(n_n, dtype=f32)[None, None, None])
    wm = (m5 + skip).transpose(2, 0, 4, 1, 3).reshape(g_n, S5_W, S5_W)

    pf = jnp.einsum('sgp,gpm->gsmp', pw[t_len - 1 - idx, 0], b_bar[0]).reshape(g_n, S5_W, p_n)
    pb = jnp.einsum('sgp,gpm->gsmp', pw[idx, 1], b_bar[1]).reshape(g_n, S5_W, p_n)
    z = jnp.zeros_like(jnp.real(pf))
    wp = jnp.concatenate([
        jnp.concatenate([jnp.real(pf), z, jnp.imag(pf), z], axis=-1),
        jnp.concatenate([z, jnp.real(pb), z, jnp.imag(pb)], axis=-1)], axis=1)

    q0 = jnp.einsum('gnp,tgp->gptn', c_mat, pw[idx + 1, 0]).reshape(g_n, p_n, S5_W)
    q1 = jnp.einsum('gnp,tgp->gptn', c_mat, pw[t_len - idx, 1]).reshape(g_n, p_n, S5_W)
    zq = jnp.zeros_like(jnp.real(q0))
    wq0 = jnp.concatenate([jnp.real(q0), zq, -jnp.imag(q0), zq], axis=1)
    wq1 = jnp.concatenate([zq, jnp.real(q1), zq, -jnp.imag(q1)], axis=1)
    lam_t = pw[t_len]
    ar = jnp.concatenate([jnp.real(lam_t[0]), jnp.real(lam_t[1])], axis=-1)[:, None, :]
    ai = jnp.concatenate([jnp.imag(lam_t[0]), jnp.imag(lam_t[1])], axis=-1)[:, None, :]
    bf = jnp.bfloat16
    return wp.astype(bf), wm.astype(bf), wq0.astype(bf), wq1.astype(bf), ar, ai


def s5_scan(u_l, u_c, weights):
    bsz, seq_len, _ = u_l.shape
    ctx_len = u_c.shape[1]

    def to_chunks(a, reverse):
        a = a.reshape(bsz, a.shape[1] // S5_T, S5_T, S5_GROUPS, S5_GROUP)
        if reverse:
            a = a[:, ::-1]
        a = a.transpose(3, 1, 0, 2, 4)
        return a.reshape(S5_GROUPS, -1, S5_W)

    ucat = jnp.concatenate([
        jnp.concatenate([to_chunks(u_c, False), to_chunks(u_l, False)], axis=1),
        jnp.concatenate([to_chunks(u_c, True), to_chunks(u_l, True)], axis=1)], axis=-1).astype(jnp.bfloat16)
    y = s5_scan_call(ucat, *weights, n_batch=bsz, ctx_chunks=ctx_len // S5_T)

    def from_chunks(a, length):
        a = a.reshape(S5_GROUPS, length // S5_T, bsz, S5_T, S5_GROUP)
        return a.transpose(2, 1, 3, 0, 4).reshape(bsz, length, D_S5)

    ctx_rows = ctx_len // S5_T * bsz
    return from_chunks(y[:, ctx_rows:], seq_len), from_chunks(y[:, :ctx_rows], ctx_len)


def rmsnorm(x, g):
    xf = x.astype(jnp.float32)
    y = xf * lax.rsqrt(jnp.mean(xf * xf, axis=-1, keepdims=True) + NORM_EPS)
    return (y * g.astype(jnp.float32)).astype(x.dtype)


def modulate(h, shift, scale):
    return h * (1 + scale) + shift


def short_conv(u, w, b, grid):
    bsz, _, ch = u.shape
    n_rows, row_len = grid
    pad = SHORT_CONV // 2
    r = jnp.pad(u.reshape(bsz, n_rows, row_len, ch), ((0, 0), (0, 0), (pad, pad), (0, 0)))
    y = b + r[:, :, 0:row_len] * w[0]
    for j in range(1, SHORT_CONV):
        y = y + r[:, :, j:j + row_len] * w[j]
    return y.reshape(bsz, n_rows * row_len, ch)


def hyena_filters(seq_len, w1, b1, w2, b2, w3, b3, sin_freq):
    t = jnp.linspace(0.0, 1.0, seq_len, dtype=jnp.float32)[:, None]
    w = 2.0 * math.pi * jnp.arange(seq_len, dtype=jnp.float32)[:, None] / seq_len
    f = jnp.linspace(1e-4, HY_BANDS - 1, HY_BANDS, dtype=jnp.float32)[None, :]
    z = jnp.concatenate([t, jnp.cos(f * w), -jnp.sin(f * w)], axis=-1)
    hdn = jnp.sin(sin_freq[0] * (z @ w1 + b1))
    hdn = jnp.sin(sin_freq[1] * (hdn @ w2 + b2))
    hf = (hdn @ w3 + b3).astype(jnp.float32).reshape(seq_len, HY_ORDER, 2, D_HY)
    deltas = jnp.abs(jnp.linspace(HY_MIN_DECAY, HY_MAX_DECAY, D_HY, dtype=jnp.float32))
    decay = jnp.exp(-t * deltas)
    return hf * decay[:, None, None, :]


def bidir_fftconv(u, h_fwd, h_bwd, skip):
    seq_len = u.shape[1]
    n_fft = 2 * seq_len
    k = jnp.concatenate([h_fwd, jnp.zeros_like(h_fwd[:1]), h_bwd[:0:-1]], axis=0)
    k = k / jnp.sum(jnp.abs(k), axis=0, keepdims=True)
    uf = u.astype(jnp.float32)
    spec = jnp.fft.rfft(uf, n=n_fft, axis=1) * jnp.fft.rfft(k, n=n_fft, axis=0)[None]
    y = jnp.fft.irfft(spec, n=n_fft, axis=1)[:, :seq_len]
    return (y + uf * skip.astype(jnp.float32)).astype(u.dtype)


def hyena_mixer(u, conv_w, conv_b, filter_params, bias, grid):
    filt = hyena_filters(u.shape[1], *filter_params)
    parts = jnp.split(short_conv(u, conv_w, conv_b, grid), 1 + HY_ORDER, axis=-1)
    z = parts[0]
    for o in range(HY_ORDER):
        z = parts[1 + o] * bidir_fftconv(z, filt[:, o, 0], filt[:, o, 1], bias[o])
    return z


def s5_discretize(lam_re, lam_im, log_dt, b_mat):
    lam = lax.complex(lam_re.astype(jnp.float32), lam_im.astype(jnp.float32))
    dt = jnp.exp(log_dt.astype(jnp.float32))[:, None]
    lam_bar = jnp.exp(lam * dt)
    b_bar = ((lam_bar - 1.0) / lam)[..., None] * b_mat
    return lam_bar, b_bar


def linear_scan(bu, lam_bar, x0):
    bu = bu.at[:, 0].add(lam_bar * x0)
    a = jnp.broadcast_to(lam_bar, (1, bu.shape[1]) + lam_bar.shape)

    def combine(left, right):
        return (left[0] * right[0], right[0] * left[1] + right[1])

    _, xs = lax.associative_scan(combine, (a, bu), axis=1)
    return xs


def s5_mixer(u_l, u_c, lam_re, lam_im, log_dt, b_re, b_im, c_re, c_im, d_skip, glu_w, need_ctx):
    bsz, seq_len, _ = u_l.shape
    ctx_len = u_c.shape[1]
    ul = u_l.astype(jnp.float32).reshape(bsz, seq_len, S5_GROUPS, S5_GROUP)
    uc = u_c.astype(jnp.float32).reshape(bsz, ctx_len, S5_GROUPS, S5_GROUP)
    b_mat = lax.complex(b_re.astype(jnp.float32), b_im.astype(jnp.float32))
    c_mat = lax.complex(c_re.astype(jnp.float32), c_im.astype(jnp.float32))
    dsk = d_skip.astype(jnp.float32).reshape(S5_GROUPS, S5_GROUP)
    x0 = jnp.zeros((bsz, S5_GROUPS, S5_STATE), jnp.complex64)
    y_l = dsk * ul
    y_c = dsk * uc if need_ctx else None
    for d in range(2):
        flip = (lambda t: jnp.flip(t, axis=1)) if d == 1 else (lambda t: t)
        lam_bar, b_bar = s5_discretize(lam_re[d], lam_im[d], log_dt[d], b_mat)
        xs_c = linear_scan(flip(jnp.einsum('blgn,gpn->blgp', uc.astype(jnp.complex64), b_bar)), lam_bar, x0)
        xs_l = flip(linear_scan(flip(jnp.einsum('blgn,gpn->blgp', ul.astype(jnp.complex64), b_bar)),
                                lam_bar, xs_c[:, -1]))
        y_l = y_l + jnp.real(jnp.einsum('blgp,gnp->blgn', xs_l, c_mat))
        if need_ctx:
            y_c = y_c + jnp.real(jnp.einsum('blgp,gnp->blgn', flip(xs_c), c_mat))

    def half_glu(y, dtype):
        g = jax.nn.gelu(y.reshape(bsz, -1, D_S5))
        return (g * jax.nn.sigmoid(g @ glu_w)).astype(dtype)

    return half_glu(y_l, u_l.dtype), (half_glu(y_c, u_c.dtype) if need_ctx else None)


def mlstm_chunkwise(q, k, v, log_i, log_f, state0, with_output):
    bsz, nh, seq_len, dh = q.shape
    nc = seq_len // ML_CHUNK
    blk = lambda t: t.reshape(bsz, nh, nc, ML_CHUNK, *t.shape[3:])
    qc, kc, vc, ic, fc = blk(q), blk(k), blk(v), blk(log_i), blk(log_f)
    b = jnp.cumsum(fc, axis=-1)
    b_end = b[..., -1]
    g = b_end[..., None] - b + ic

    def chunk_step(carry, inp):
        c_mem, n_mem, m_mem = carry
        k_t, v_t, g_t, be = inp
        m_new = jnp.maximum(be + m_mem, jnp.max(g_t, axis=-1))
        decay = jnp.exp(be + m_mem - m_new)
        w = jnp.exp(g_t - m_new[..., None])
        c_new = decay[..., None, None] * c_mem + jnp.einsum('bhs,bhsd,bhse->bhde', w, v_t, k_t)
        n_new = decay[..., None] * n_mem + jnp.einsum('bhs,bhse->bhe', w, k_t)
        return (c_new, n_new, m_new), (c_mem, n_mem, m_mem)

    to_scan = lambda t: jnp.moveaxis(t, 2, 0)
    final, (c_prev, n_prev, m_prev) = lax.scan(
        chunk_step, state0, (to_scan(kc), to_scan(vc), to_scan(g), to_scan(b_end)))
    if not with_output:
        return None, final
    c_prev = jnp.moveaxis(c_prev, 0, 2)
    n_prev = jnp.moveaxis(n_prev, 0, 2)
    m_prev = jnp.moveaxis(m_prev, 0, 2)
    lower = jnp.tril(jnp.ones((ML_CHUNK, ML_CHUNK), dtype=bool))
    a = b + m_prev[..., None]
    dmat = jnp.where(lower, b[..., :, None] - b[..., None, :] + ic[..., None, :], NEG)
    m_out = jnp.maximum(a, jnp.max(dmat, axis=-1))
    wmat = jnp.exp(dmat - m_out[..., None])
    wa = jnp.exp(a - m_out)
    s = jnp.einsum('bhctd,bhcsd->bhcts', qc, kc) * wmat
    num = jnp.einsum('bhcts,bhcsd->bhctd', s, vc) + wa[..., None] * jnp.einsum('bhcde,bhcte->bhctd', c_prev, qc)
    den = jnp.sum(s, axis=-1) + wa * jnp.einsum('bhce,bhcte->bhct', n_prev, qc)
    h = num / jnp.maximum(jnp.abs(den), jnp.exp(-m_out))[..., None]
    return h.reshape(bsz, nh, seq_len, dh), final


def mlstm_heads(qk, v, gates, gate_b, conv_w, conv_b, grid):
    bsz, seq_len, _ = v.shape
    qk = jax.nn.silu(short_conv(qk, conv_w, conv_b, grid)).astype(jnp.float32)
    q, k = jnp.split(qk, 2, axis=-1)
    to_heads = lambda t: t.reshape(bsz, seq_len, ML_HEADS, ML_HEAD_DIM).transpose(0, 2, 1, 3)
    g = (gates + gate_b).astype(jnp.float32).reshape(bsz, seq_len, 4, ML_HEADS).transpose(2, 0, 3, 1)
    return to_heads(q), to_heads(k) * ML_HEAD_DIM ** -0.5, to_heads(v.astype(jnp.float32)), g


def mlstm_head_out(h, o, norm_g):
    bsz, _, seq_len, _ = h.shape
    h = h.transpose(0, 2, 1, 3)
    h = h * lax.rsqrt(jnp.mean(h * h, axis=-1, keepdims=True) + NORM_EPS)
    h = h.reshape(bsz, seq_len, D_ML) * norm_g.astype(jnp.float32)
    return (h * jax.nn.sigmoid(o.astype(jnp.float32))).astype(o.dtype)


def mlstm_mixer(qk_l, v_l, gt_l, o_l, qk_c, v_c, gt_c, o_c, gate_b, conv_w, conv_b, norm_g,
                lat_grid, ctx_grid, need_ctx):
    ql, kl, vl, gl = mlstm_heads(qk_l, v_l, gt_l, gate_b, conv_w, conv_b, lat_grid)
    qc, kc, vc, gc = mlstm_heads(qk_c, v_c, gt_c, gate_b, conv_w, conv_b, ctx_grid)
    bsz = ql.shape[0]
    state0 = (jnp.zeros((bsz, ML_HEADS, ML_HEAD_DIM, ML_HEAD_DIM), jnp.float32),
              jnp.zeros((bsz, ML_HEADS, ML_HEAD_DIM), jnp.float32),
              jnp.full((bsz, ML_HEADS), NEG, jnp.float32))
    h_l = jnp.zeros_like(ql)
    h_c = jnp.zeros_like(qc) if need_ctx else None
    for d in range(2):
        flip = (lambda t: jnp.flip(t, axis=2)) if d == 1 else (lambda t: t)
        hc_d, state_c = mlstm_chunkwise(flip(qc), flip(kc), flip(vc), flip(gc[2 * d]),
                                        flip(jax.nn.log_sigmoid(gc[2 * d + 1])), state0, need_ctx)
        hl_d, _ = mlstm_chunkwise(flip(ql), flip(kl), flip(vl), flip(gl[2 * d]),
                                  flip(jax.nn.log_sigmoid(gl[2 * d + 1])), state_c, True)
        h_l = h_l + flip(hl_d)
        if need_ctx:
            h_c = h_c + flip(hc_d)
    out_l = mlstm_head_out(h_l, o_l, norm_g)
    out_c = mlstm_head_out(h_c, o_c, norm_g) if need_ctx else None
    return out_l, out_c


def split_proj(p):
    out, start = [], 0
    for s in PROJ_SIZES:
        out.append(p[..., start:start + s])
        start += s
    s5, qk, v, hy, o, mg, gt = out
    return s5, qk, v, gt[..., :4 * ML_HEADS], hy, o, mg


def permute_w_in(w_in):
    cols, start = [], 0
    for s in COL_SIZES:
        cols.append(w_in[:, start:start + s])
        start += s
    s5, qk, v, gt, hy, o, mg = cols
    gt = jnp.pad(gt, ((0, 0), (0, GATE_PAD - gt.shape[1])))
    return jnp.concatenate([s5, qk, v, hy, o, mg, gt], axis=1).astype(jnp.bfloat16)


def moe(t, router_w, router_b, w1g, w1l, b1g, b1l, w2, b2):
    n_tok, d = t.shape
    logits = (t @ router_w + router_b).astype(jnp.float32)
    top_val, top_idx = lax.top_k(logits, TOP_K)
    weights = jax.nn.softmax(top_val, axis=-1).astype(t.dtype)
    flat_e = top_idx.reshape(-1)
    n_assign = flat_e.shape[0]
    n_blocks = -(-n_assign // MOE_ROWS) + N_EXPERTS
    n_slots = n_blocks * MOE_ROWS
    order = jnp.argsort(flat_e, stable=True)
    e_sorted = flat_e[order]
    counts = jnp.bincount(flat_e, length=N_EXPERTS).astype(jnp.int32)
    padded = (counts + MOE_ROWS - 1) // MOE_ROWS * MOE_ROWS
    end_pad = jnp.cumsum(padded)
    start_pad = end_pad - padded
    start = jnp.cumsum(counts) - counts
    dest = start_pad[e_sorted] + jnp.arange(n_assign, dtype=jnp.int32) - start[e_sorted]
    slot_tok = jnp.zeros((n_slots,), jnp.int32).at[dest].set((order // TOP_K).astype(jnp.int32))
    slot_w = jnp.zeros((n_slots,), t.dtype).at[dest].set(weights.reshape(-1)[order])
    block_start = jnp.arange(n_blocks, dtype=jnp.int32) * MOE_ROWS
    block_e = jnp.minimum(jnp.searchsorted(end_pad, block_start, side='right'), N_EXPERTS - 1).astype(jnp.int32)
    n_used = (end_pad[-1] // MOE_ROWS).astype(jnp.int32).reshape(1)
    pos = jnp.zeros((n_assign,), jnp.int32).at[order].set(dest)
    x_sorted = t.astype(jnp.bfloat16)[slot_tok]
    out = moe_expert_blocks(block_e, n_used, x_sorted, slot_w.reshape(n_slots, 1),
                            w1g, w1l, b1g, b1l, w2, b2)
    return out[pos].reshape(n_tok, TOP_K, d).sum(axis=1)


def kernel(x, c, ctx, c_ctx, ada_w, ada_b, norm1_g, norm2_g, final_norm_g, w_in,
           hy_conv_w, hy_conv_b, hy_f_w1, hy_f_b1, hy_f_w2, hy_f_b2, hy_f_w3, hy_f_b3, hy_sin_freq, hy_bias,
           s5_lam_re, s5_lam_im, s5_log_dt, s5_b_re, s5_b_im, s5_c_re, s5_c_im, s5_d, s5_glu_w,
           ml_conv_w, ml_conv_b, ml_gate_b, ml_norm_g,
           w_br_hy, w_br_s5, w_br_ml, w_out,
           moe_router_w, moe_router_b, moe_w1, moe_b1, moe_w2, moe_b2):
    bsz, seq_len, d = x.shape
    ctx_len = ctx.shape[1]
    rows = seq_len // GRID_W
    lat_grid = (rows, GRID_W)
    ctx_grid = (1, ctx_len)
    bf = jnp.bfloat16
    silu_c = jax.nn.silu(c.astype(jnp.float32))
    silu_cc = jax.nn.silu(c_ctx.astype(jnp.float32))[None]
    for l in range(DEPTH):
        need_ctx = l < DEPTH - 1
        mod_l = (silu_c @ ada_w[l] + ada_b[l]).astype(x.dtype)[:, None, :]
        mod_c = (silu_cc @ ada_w[l] + ada_b[l]).astype(ctx.dtype)[:, None, :]
        sh1_l, sc1_l, g1_l, sh2_l, sc2_l, g2_l = jnp.split(mod_l, 6, axis=-1)
        sh1_c, sc1_c, g1_c, sh2_c, sc2_c, g2_c = jnp.split(mod_c, 6, axis=-1)

        w_in_p = permute_w_in(w_in[l])
        gain1 = norm1_g[l].reshape(1, d)
        p_l = norm_mod_project(x.reshape(bsz * seq_len, d), sh1_l, sc1_l, gain1, w_in_p, seq_len, PROJ_TM)
        p_c = norm_mod_project(ctx.reshape(bsz * ctx_len, d), sh1_c, sc1_c, gain1, w_in_p,
                               bsz * ctx_len, PROJ_TM)
        s5_l, qk_l, v_l, gt_l, hy_l, o_l, mg_l = split_proj(p_l.reshape(bsz, seq_len, PROJ_COLS))
        s5_c, qk_c, v_c, gt_c, hy_c, o_c, mg_c = split_proj(p_c.reshape(bsz, ctx_len, PROJ_COLS))

        hy_params = (hy_f_w1[l], hy_f_b1[l], hy_f_w2[l], hy_f_b2[l], hy_f_w3[l], hy_f_b3[l], hy_sin_freq[l])
        hy_out_l = hyena_mixer(hy_l, hy_conv_w[l], hy_conv_b[l], hy_params, hy_bias[l], lat_grid)
        s5_w = s5_chunk_weights(s5_lam_re[l], s5_lam_im[l], s5_log_dt[l], s5_b_re[l], s5_b_im[l],
                                s5_c_re[l], s5_c_im[l], s5_d[l])
        s5_out_l, s5_out_c = s5_scan(s5_l, s5_c, s5_w)
        wglu = s5_glu_w[l].astype(bf)
        ml_out_l, ml_out_c = mlstm_mixer(qk_l, v_l, gt_l, o_l, qk_c, v_c, gt_c, o_c, ml_gate_b[l],
                                         ml_conv_w[l], ml_conv_b[l], ml_norm_g[l], lat_grid, ctx_grid, need_ctx)

        wh, ws, wm, wo = w_br_hy[l].astype(bf), w_br_s5[l].astype(bf), w_br_ml[l].astype(bf), w_out[l].astype(bf)
        flat = lambda a: a.reshape(-1, a.shape[-1])
        x = merge_project_residual(flat(hy_out_l), flat(s5_out_l), flat(ml_out_l), flat(mg_l), flat(x), g1_l,
                                   wglu, wh, ws, wm, wo, seq_len, MERGE_TM).reshape(bsz, seq_len, d)
        h2_l = modulate(rmsnorm(x, norm2_g[l]), sh2_l, sc2_l)

        w1 = moe_w1[l]
        moe_args = (moe_router_w[l], moe_router_b[l],
                    w1[:, :, 0::2].astype(bf), w1[:, :, 1::2].astype(bf),
                    moe_b1[l][:, None, 0::2], moe_b1[l][:, None, 1::2],
                    moe_w2[l].astype(bf), moe_b2[l][:, None, :])
        if need_ctx:
            hy_out_c = hyena_mixer(hy_c, hy_conv_w[l], hy_conv_b[l], hy_params, hy_bias[l], ctx_grid)
            ctx = merge_project_residual(flat(hy_out_c), flat(s5_out_c), flat(ml_out_c), flat(mg_c), flat(ctx),
                                         g1_c, wglu, wh, ws, wm, wo, bsz * ctx_len, MERGE_TM).reshape(bsz, ctx_len, d)
            h2_c = modulate(rmsnorm(ctx, norm2_g[l]), sh2_c, sc2_c)
            tok = jnp.concatenate([h2_c, h2_l], axis=1).reshape(-1, d)
            f = moe(tok, *moe_args).reshape(bsz, ctx_len + seq_len, d)
            ctx = ctx + g2_c * f[:, :ctx_len]
            x = x + g2_l * f[:, ctx_len:]
        else:
            x = x + g2_l * moe(h2_l.reshape(-1, d), *moe_args).reshape(bsz, seq_len, d)
    return rmsnorm(x, final_norm_g)
```

```python
import functools
import math

import jax
import jax.numpy as jnp
from jax import lax
from jax.experimental import pallas as pl
from jax.experimental.pallas import tpu as pltpu

D_MODEL = 1024
DEPTH = 4
GRID_W = 64

D_HY = 384
D_S5 = 384
D_ML = 384
N_BRANCH = 3
SHORT_CONV = 3

HY_ORDER = 2
HY_EMB = 33
HY_BANDS = (HY_EMB - 1) // 2
HY_DECAY_TARGET = 1e-2
HY_FAST_DECAY_PCT = 0.3
HY_SLOW_DECAY_PCT = 1.5
HY_MIN_DECAY = math.log(HY_DECAY_TARGET) / HY_SLOW_DECAY_PCT
HY_MAX_DECAY = math.log(HY_DECAY_TARGET) / HY_FAST_DECAY_PCT

S5_GROUP = 16
S5_GROUPS = D_S5 // S5_GROUP
S5_STATE = 64

ML_HEADS = 4
ML_HEAD_DIM = D_ML // ML_HEADS
NEG = -1e30

N_EXPERTS = 32
TOP_K = 4
D_FF_EXPERT = 512
SWIGLU_LIMIT = 7.0
SWIGLU_ALPHA = 1.702

NORM_EPS = 1e-6

COL_SIZES = (D_S5, 2 * D_ML, D_ML, 4 * ML_HEADS, (1 + HY_ORDER) * D_HY, D_ML, N_BRANCH * D_MODEL)

LANES = 128
VMEM_LIMIT_BYTES = 56 * 1024 * 1024

ML_HP = LANES
ML_DP = ML_HEADS * ML_HP
ML_TC = 128
GATE_PAD = LANES

PROJ_OUT = (("s5", D_S5, jnp.bfloat16), ("qk", 2 * ML_DP, jnp.bfloat16), ("v", ML_DP, jnp.bfloat16),
            ("hy", (1 + HY_ORDER) * D_HY, jnp.bfloat16), ("o", ML_DP, jnp.bfloat16),
            ("mg", N_BRANCH * D_MODEL, jnp.bfloat16), ("gt", GATE_PAD, jnp.float32))
PROJ_COLS = sum(w for _, w, _ in PROJ_OUT)
PROJ_TM = 512

MOE_ROWS = 512
MERGE_TM = 512
ROUTER_PAD = LANES

HY_CB = 8
HY_CH = 128


def _proj_kernel(x_ref, shift_ref, scale_ref, g_ref, w_ref, *o_refs):
    x = x_ref[...]
    ms = jnp.mean(x * x, axis=-1, keepdims=True)
    y = x * lax.rsqrt(ms + NORM_EPS) * g_ref[...]
    h = (y * (1.0 + scale_ref[0]) + shift_ref[0]).astype(jnp.bfloat16)
    start = 0
    for o_ref, (_, width, _) in zip(o_refs, PROJ_OUT):
        o_ref[...] = jnp.dot(h, w_ref[:, start:start + width],
                             preferred_element_type=jnp.float32).astype(o_ref.dtype)
        start += width


def norm_mod_project(x2d, shift, scale, gain, w_bf16, rows_per_mod):
    rows, d = x2d.shape
    tm = PROJ_TM
    blocks_per_mod = rows_per_mod // tm
    return pl.pallas_call(
        _proj_kernel,
        grid=(rows // tm,),
        in_specs=[
            pl.BlockSpec((tm, d), lambda i: (i, 0)),
            pl.BlockSpec((1, 1, d), lambda i: (i // blocks_per_mod, 0, 0)),
            pl.BlockSpec((1, 1, d), lambda i: (i // blocks_per_mod, 0, 0)),
            pl.BlockSpec((1, d), lambda i: (0, 0)),
            pl.BlockSpec((d, PROJ_COLS), lambda i: (0, 0)),
        ],
        out_specs=[pl.BlockSpec((tm, w), lambda i: (i, 0)) for _, w, _ in PROJ_OUT],
        out_shape=[jax.ShapeDtypeStruct((rows, w), dt) for _, w, dt in PROJ_OUT],
        compiler_params=pltpu.CompilerParams(
            dimension_semantics=("parallel",),
            vmem_limit_bytes=VMEM_LIMIT_BYTES),
        name="norm_mod_project",
    )(x2d, shift, scale, gain, w_bf16)


def pad_heads(w):
    lead = w.shape[:-1]
    w = w.reshape(lead + (ML_HEADS, ML_HEAD_DIM))
    w = jnp.pad(w, [(0, 0)] * len(lead) + [(0, 0), (0, ML_HP - ML_HEAD_DIM)])
    return w.reshape(lead + (ML_DP,))


def permute_w_in(w_in):
    cols, start = [], 0
    for s in COL_SIZES:
        cols.append(w_in[:, start:start + s])
        start += s
    s5, qk, v, gt, hy, o, mg = cols
    qk = jnp.concatenate([pad_heads(qk[:, :D_ML]), pad_heads(qk[:, D_ML:])], axis=1)
    gt = jnp.pad(gt, ((0, 0), (0, GATE_PAD - gt.shape[1])))
    return jnp.concatenate([s5, qk, pad_heads(v), hy, pad_heads(o), mg, gt], axis=1).astype(jnp.bfloat16)


def _merge_kernel(hy_ref, s5_ref, ml_ref, mg_ref, x_ref, g1_ref, sh2_ref, sc2_ref, n2_ref,
                  wglu_ref, wh_ref, ws_ref, wm_ref, wo_ref, rw_ref, rb_ref,
                  xo_ref, h2_ref, lg_ref):
    bf = jnp.bfloat16
    f32 = jnp.float32
    d = D_MODEL
    g = jax.nn.gelu(s5_ref[...])
    s5 = g * jax.nn.sigmoid(jnp.dot(g.astype(bf), wglu_ref[...], preferred_element_type=f32))
    y = jax.nn.sigmoid(mg_ref[:, 0:d].astype(f32)) * jnp.dot(hy_ref[...], wh_ref[...], preferred_element_type=f32)
    y = y + jax.nn.sigmoid(mg_ref[:, d:2 * d].astype(f32)) * jnp.dot(s5.astype(bf), ws_ref[...],
                                                                     preferred_element_type=f32)
    y = y + jax.nn.sigmoid(mg_ref[:, 2 * d:3 * d].astype(f32)) * jnp.dot(ml_ref[...], wm_ref[...],
                                                                         preferred_element_type=f32)
    out = jnp.dot(y.astype(bf), wo_ref[...], preferred_element_type=f32)
    xn = x_ref[...] + g1_ref[0] * out
    xo_ref[...] = xn
    ms = jnp.mean(xn * xn, axis=-1, keepdims=True)
    h2 = (xn * lax.rsqrt(ms + NORM_EPS) * n2_ref[...]) * (1.0 + sc2_ref[0]) + sh2_ref[0]
    h2b = h2.astype(bf)
    h2_ref[...] = h2b
    lg_ref[...] = jnp.dot(h2b, rw_ref[...], preferred_element_type=f32) + rb_ref[...]


def merge_project_residual(hy, s5, ml, mg, x2d, g1, sh2, sc2, n2, wglu, wh, ws, wm, wo, rw, rb, rows_per_mod):
    rows, d = x2d.shape
    tm = MERGE_TM
    blocks_per_mod = rows_per_mod // tm
    row_spec = lambda c: pl.BlockSpec((tm, c), lambda i: (i, 0))
    mod_spec = pl.BlockSpec((1, 1, d), lambda i: (i // blocks_per_mod, 0, 0))
    full = lambda a: pl.BlockSpec(a.shape, lambda i: (0, 0))
    return pl.pallas_call(
        _merge_kernel,
        grid=(rows // tm,),
        in_specs=[row_spec(D_HY), row_spec(D_S5), row_spec(ML_DP), row_spec(N_BRANCH * d), row_spec(d),
                  mod_spec, mod_spec, mod_spec, full(n2),
                  full(wglu), full(wh), full(ws), full(wm), full(wo), full(rw), full(rb)],
        out_specs=[row_spec(d), row_spec(d), row_spec(ROUTER_PAD)],
        out_shape=[jax.ShapeDtypeStruct((rows, d), jnp.float32),
                   jax.ShapeDtypeStruct((rows, d), jnp.bfloat16),
                   jax.ShapeDtypeStruct((rows, ROUTER_PAD), jnp.float32)],
        compiler_params=pltpu.CompilerParams(
            dimension_semantics=("parallel",),
            vmem_limit_bytes=VMEM_LIMIT_BYTES),
        name="merge_project_residual",
    )(hy, s5, ml, mg, x2d, g1, sh2, sc2, n2, wglu, wh, ws, wm, wo, rw, rb)


def _moe_kernel(be_ref, nu_ref, x_ref, w1g_ref, w1l_ref, b1g_ref, b1l_ref, w2_ref, b2_ref, o_ref):
    @pl.when(pl.program_id(0) < nu_ref[0])
    def _():
        x = x_ref[...]
        hg = jnp.dot(x, w1g_ref[0], preferred_element_type=jnp.float32) + b1g_ref[0]
        hl = jnp.dot(x, w1l_ref[0], preferred_element_type=jnp.float32) + b1l_ref[0]
        x_glu = jnp.minimum(hg, SWIGLU_LIMIT)
        x_lin = jnp.clip(hl, -SWIGLU_LIMIT, SWIGLU_LIMIT)
        act = x_glu * jax.nn.sigmoid(SWIGLU_ALPHA * x_glu) * (x_lin + 1.0)
        out = jnp.dot(act.astype(jnp.bfloat16), w2_ref[0], preferred_element_type=jnp.float32) + b2_ref[0]
        o_ref[...] = out.astype(o_ref.dtype)


def moe_expert_blocks(block_e, n_used, x_sorted, w1g, w1l, b1g, b1l, w2, b2):
    n_slots, d = x_sorted.shape
    n_blocks = n_slots // MOE_ROWS
    f = D_FF_EXPERT

    def row_map(i, be, nu):
        return (jnp.minimum(i, nu[0] - 1), 0)

    def w_map(i, be, nu):
        return (be[jnp.minimum(i, nu[0] - 1)], 0, 0)

    grid_spec = pltpu.PrefetchScalarGridSpec(
        num_scalar_prefetch=2,
        grid=(n_blocks,),
        in_specs=[
            pl.BlockSpec((MOE_ROWS, d), row_map),
            pl.BlockSpec((1, d, f), w_map),
            pl.BlockSpec((1, d, f), w_map),
            pl.BlockSpec((1, 1, f), w_map),
            pl.BlockSpec((1, 1, f), w_map),
            pl.BlockSpec((1, f, d), w_map),
            pl.BlockSpec((1, 1, d), w_map),
        ],
        out_specs=pl.BlockSpec((MOE_ROWS, d), row_map),
    )
    return pl.pallas_call(
        _moe_kernel,
        grid_spec=grid_spec,
        out_shape=jax.ShapeDtypeStruct((n_slots, d), jnp.bfloat16),
        compiler_params=pltpu.CompilerParams(
            dimension_semantics=("arbitrary",),
            vmem_limit_bytes=VMEM_LIMIT_BYTES),
        name="moe_expert_blocks",
    )(block_e, n_used, x_sorted, w1g, w1l, b1g, b1l, w2, b2)


S5_T = 16
S5_W = S5_T * S5_GROUP
S5_HALF = 2 * S5_STATE


def _s5_kernel(u_ref, wp_ref, wm_ref, wq0_ref, wq1_ref, ar_ref, ai_ref, y_ref, s_scr, x_scr, yb_scr,
               *, n_batch, ctx_chunks, lat_chunks):
    nb = n_batch
    u = u_ref[0]
    s_scr[...] = jnp.dot(u, wp_ref[0], preferred_element_type=jnp.float32)
    ar = jnp.broadcast_to(ar_ref[0], (nb, S5_HALF))
    ai = jnp.broadcast_to(ai_ref[0], (nb, S5_HALF))

    def step(i, carry):
        xr, xi = carry
        r = pl.multiple_of(i * nb, nb)
        x_scr[pl.ds(r, nb), 0:S5_HALF] = xr
        x_scr[pl.ds(r, nb), S5_HALF:2 * S5_HALF] = xi
        sr = s_scr[pl.ds(r, nb), 0:S5_HALF]
        si = s_scr[pl.ds(r, nb), S5_HALF:2 * S5_HALF]
        return ar * xr - ai * xi + sr, ar * xi + ai * xr + si

    zero = jnp.zeros((nb, S5_HALF), jnp.float32)
    lax.fori_loop(0, ctx_chunks + lat_chunks, step, (zero, zero))

    xin = x_scr[...].astype(jnp.bfloat16)
    y_ref[0] = (jnp.dot(u[:, 0:S5_W], wm_ref[0], preferred_element_type=jnp.float32)
                + jnp.dot(xin, wq0_ref[0], preferred_element_type=jnp.float32))
    yb_scr[...] = jnp.dot(xin, wq1_ref[0], preferred_element_type=jnp.float32)
    for seg_start, seg_chunks in ((0, ctx_chunks), (ctx_chunks * nb, lat_chunks)):
        for i in range(seg_chunks):
            dst = seg_start + nb * i
            src = seg_start + nb * (seg_chunks - 1 - i)
            y_ref[0, dst:dst + nb, :] += yb_scr[src:src + nb, :]


def s5_scan_call(ucat, wp, wm, wq0, wq1, ar, ai, *, n_batch, ctx_chunks):
    groups, rows, _ = ucat.shape
    lat_chunks = rows // n_batch - ctx_chunks
    per_group = lambda a: pl.BlockSpec((1,) + a.shape[1:], lambda g: (g, 0, 0))
    return pl.pallas_call(
        functools.partial(_s5_kernel, n_batch=n_batch, ctx_chunks=ctx_chunks, lat_chunks=lat_chunks),
        grid=(groups,),
        in_specs=[per_group(a) for a in (ucat, wp, wm, wq0, wq1, ar, ai)],
        out_specs=pl.BlockSpec((1, rows, S5_W), lambda g: (g, 0, 0)),
        out_shape=jax.ShapeDtypeStruct((groups, rows, S5_W), jnp.float32),
        scratch_shapes=[pltpu.VMEM((rows, S5_W), jnp.float32)] * 3,
        compiler_params=pltpu.CompilerParams(
            dimension_semantics=("parallel",),
            vmem_limit_bytes=VMEM_LIMIT_BYTES),
        name="s5_scan",
    )(ucat, wp, wm, wq0, wq1, ar, ai)


def s5_chunk_weights(lam_re, lam_im, log_dt, b_re, b_im, c_re, c_im, d_skip):
    f32 = jnp.float32
    t_len, g_n, p_n, n_n = S5_T, S5_GROUPS, S5_STATE, S5_GROUP
    b_mat = lax.complex(b_re.astype(f32), b_im.astype(f32))
    c_mat = lax.complex(c_re.astype(f32), c_im.astype(f32))
    lam = lax.complex(lam_re.astype(f32), lam_im.astype(f32))
    lam_dt = lam * jnp.exp(log_dt.astype(f32))[..., None]
    b_bar = ((jnp.exp(lam_dt) - 1.0) / lam)[..., None] * b_mat
    j = jnp.arange(t_len + 1, dtype=f32)
    pw = jnp.exp(j[:, None, None, None] * lam_dt[None])

    kern = [jnp.real(jnp.einsum('gnp,jgp,gpm->jgnm', c_mat, pw[:t_len, d], b_bar[d])) for d in range(2)]
    idx = jnp.arange(t_len)
    diff = idx[None, :] - idx[:, None]
    k0 = kern[0][jnp.clip(diff, 0, None)]
    k1 = kern[1][jnp.clip(-diff, 0, None)]
    m5 = (jnp.where((diff >= 0)[:, :, None, None, None], k0, 0.0)
          + jnp.where((diff <= 0)[:, :, None, None, None], k1, 0.0))
    skip = jnp.eye(t_len, dtype=f32)[:, :, None, None, None] * (
        d_skip.astype(f32).reshape(g_n, n_n)[None, None, :, :, None] * jnp.eye(n_n, dtype=f32)[None, None, None])
    wm = (m5 + skip).transpose(2, 0, 4, 1, 3).reshape(g_n, S5_W, S5_W)

    pf = jnp.einsum('sgp,gpm->gsmp', pw[t_len - 1 - idx, 0], b_bar[0]).reshape(g_n, S5_W, p_n)
    pb = jnp.einsum('sgp,gpm->gsmp', pw[idx, 1], b_bar[1]).reshape(g_n, S5_W, p_n)
    z = jnp.zeros_like(jnp.real(pf))
    wp = jnp.concatenate([
        jnp.concatenate([jnp.real(pf), z, jnp.imag(pf), z], axis=-1),
        jnp.concatenate([z, jnp.real(pb), z, jnp.imag(pb)], axis=-1)], axis=1)

    q0 = jnp.einsum('gnp,tgp->gptn', c_mat, pw[idx + 1, 0]).reshape(g_n, p_n, S5_W)
    q1 = jnp.einsum('gnp,tgp->gptn', c_mat, pw[t_len - idx, 1]).reshape(g_n, p_n, S5_W)
    zq = jnp.zeros_like(jnp.real(q0))
    wq0 = jnp.concatenate([jnp.real(q0), zq, -jnp.imag(q0), zq], axis=1)
    wq1 = jnp.concatenate([zq, jnp.real(q1), zq, -jnp.imag(q1)], axis=1)
    lam_t = pw[t_len]
    ar = jnp.concatenate([jnp.real(lam_t[0]), jnp.real(lam_t[1])], axis=-1)[:, None, :]
    ai = jnp.concatenate([jnp.imag(lam_t[0]), jnp.imag(lam_t[1])], axis=-1)[:, None, :]
    bf = jnp.bfloat16
    return wp.astype(bf), wm.astype(bf), wq0.astype(bf), wq1.astype(bf), ar, ai


def s5_scan(u_l, u_c, weights):
    bsz, seq_len, _ = u_l.shape
    ctx_len = u_c.shape[1]

    def to_chunks(a, reverse):
        a = a.reshape(bsz, a.shape[1] // S5_T, S5_T, S5_GROUPS, S5_GROUP)
        if reverse:
            a = a[:, ::-1]
        a = a.transpose(3, 1, 0, 2, 4)
        return a.reshape(S5_GROUPS, -1, S5_W)

    ucat = jnp.concatenate([
        jnp.concatenate([to_chunks(u_c, False), to_chunks(u_l, False)], axis=1),
        jnp.concatenate([to_chunks(u_c, True), to_chunks(u_l, True)], axis=1)], axis=-1).astype(jnp.bfloat16)
    y = s5_scan_call(ucat, *weights, n_batch=bsz, ctx_chunks=ctx_len // S5_T)

    def from_chunks(a, length):
        a = a.reshape(S5_GROUPS, length // S5_T, bsz, S5_T, S5_GROUP)
        return a.transpose(2, 1, 3, 0, 4).reshape(bsz, length, D_S5)

    ctx_rows = ctx_len // S5_T * bsz
    return from_chunks(y[:, ctx_rows:], seq_len), from_chunks(y[:, :ctx_rows], ctx_len)


def _dot01(a01, x):
    bf, f32 = jnp.bfloat16, jnp.float32
    hi = x.astype(bf)
    r1 = x - hi.astype(f32)
    mid = r1.astype(bf)
    lo = (r1 - mid.astype(f32)).astype(bf)
    d = lambda y: jnp.dot(a01, y, preferred_element_type=f32)
    return d(hi) + d(mid) + d(lo)


def _mlstm_kernel(qkc_ref, qkl_ref, vc_ref, vl_ref, gc_ref, gl_ref, oc_ref, ol_ref,
                  cw_ref, cb_ref, gb_ref, ng_ref,
                  outc_ref, outl_ref,
                  q_scr, k_scr, v_scr, g_scr, h_scr, c_scr, n_scr, m_scr,
                  *, ctx_len, seq_len, row_len):
    bf, f32 = jnp.bfloat16, jnp.float32
    tc = ML_TC
    nc_ctx = ctx_len // tc
    nc_lat = seq_len // tc
    nc_tot = nc_ctx + nc_lat

    def conv_silu(x, n_rows, rlen):
        row = lax.broadcasted_iota(jnp.int32, (n_rows, 1), 0) % rlen
        prev = jnp.where(row == 0, 0.0, pltpu.roll(x, 1, 0))
        nxt = jnp.where(row == rlen - 1, 0.0, pltpu.roll(x, n_rows - 1, 0))
        y = cb_ref[...] + prev * cw_ref[0:1, :] + x * cw_ref[1:2, :] + nxt * cw_ref[2:3, :]
        return y * jax.nn.sigmoid(y)

    kscale = ML_HEAD_DIM ** -0.5
    a = conv_silu(qkc_ref[0].astype(f32), ctx_len, ctx_len)
    q_scr[0:ctx_len, :] = a[:, 0:ML_DP].astype(bf)
    k_scr[0:ctx_len, :] = (a[:, ML_DP:2 * ML_DP] * kscale).astype(bf)
    v_scr[0:ctx_len, :] = vc_ref[0]
    g_scr[0:ctx_len, :] = gc_ref[0] + gb_ref[...]

    def prep(c, _):
        r = pl.multiple_of(c * tc, tc)
        a = conv_silu(qkl_ref[0, pl.ds(r, tc), :].astype(f32), tc, row_len)
        q_scr[pl.ds(ctx_len + r, tc), :] = a[:, 0:ML_DP].astype(bf)
        k_scr[pl.ds(ctx_len + r, tc), :] = (a[:, ML_DP:2 * ML_DP] * kscale).astype(bf)
        v_scr[pl.ds(ctx_len + r, tc), :] = vl_ref[0, pl.ds(r, tc), :]
        g_scr[pl.ds(ctx_len + r, tc), :] = gl_ref[0, pl.ds(r, tc), :] + gb_ref[...]
        return 0

    lax.fori_loop(0, nc_lat, prep, 0)

    ti = lax.broadcasted_iota(jnp.int32, (tc, tc), 0)
    si = lax.broadcasted_iota(jnp.int32, (tc, tc), 1)

    for d in range(2):
        causal = (si <= ti) if d == 0 else (si >= ti)
        tri = causal.astype(bf)
        c_scr[...] = jnp.zeros_like(c_scr)
        n_scr[...] = jnp.zeros_like(n_scr)
        m_scr[...] = jnp.full_like(m_scr, NEG)

        def chunk(i, _, d=d, causal=causal, tri=tri):
            if d == 0:
                ci = i
            else:
                ci = jnp.where(i < nc_ctx, nc_ctx - 1 - i, nc_tot + nc_ctx - 1 - i)
            r0 = pl.multiple_of(ci * tc, tc)
            gts = g_scr[pl.ds(r0, tc), :]
            logf = jax.nn.log_sigmoid(gts)
            bm = _dot01(tri, logf)
            gts_t = gts.T
            bm_t = bm.T
            b_end_row = bm[tc - 1:tc, :] if d == 0 else bm[0:1, :]
            for h in range(ML_HEADS):
                icol = h + 8 * d
                fcol = ML_HEADS + h + 8 * d
                lanes = slice(h * ML_HP, (h + 1) * ML_HP)
                i_col = gts[:, icol:icol + 1]
                b_col = bm[:, fcol:fcol + 1]
                i_row = gts_t[icol:icol + 1, :]
                b_row = bm_t[fcol:fcol + 1, :]
                b_end = b_end_row[:, fcol:fcol + 1]
                m_prev = m_scr[h:h + 1, 0:1]
                dmat = jnp.where(causal, b_col - b_row + i_row, NEG)
                a_col = b_col + m_prev
                m_out = jnp.maximum(a_col, jnp.max(dmat, axis=1, keepdims=True))
                wmat = jnp.exp(dmat - m_out)
                wa = jnp.exp(a_col - m_out)
                q = q_scr[pl.ds(r0, tc), lanes]
                k = k_scr[pl.ds(r0, tc), lanes]
                v = v_scr[pl.ds(r0, tc), lanes]
                s = lax.dot_general(q, k, (((1,), (1,)), ((), ())), preferred_element_type=f32) * wmat
                c_mem = c_scr[h * ML_HP:(h + 1) * ML_HP, :]
                n_row = n_scr[h:h + 1, :]
                qf = q.astype(f32)
                num = (jnp.dot(s.astype(bf), v, preferred_element_type=f32)
                       + wa * lax.dot_general(q, c_mem.astype(bf), (((1,), (1,)), ((), ())),
                                              preferred_element_type=f32))
                den = jnp.sum(s, axis=1, keepdims=True) + wa * jnp.sum(qf * n_row, axis=1, keepdims=True)
                hval = num / jnp.maximum(jnp.abs(den), jnp.exp(-m_out))
                if d == 0:
                    h_scr[pl.ds(r0, tc), lanes] = hval
                else:
                    h_scr[pl.ds(r0, tc), lanes] += hval
                g_col = b_end - b_col + i_col
                m_new = jnp.maximum(b_end + m_prev, jnp.max(g_col, axis=0, keepdims=True))
                decay = jnp.exp(b_end + m_prev - m_new)
                w_col = jnp.exp(g_col - m_new)
                vw = (v.astype(f32) * w_col).astype(bf)
                c_scr[h * ML_HP:(h + 1) * ML_HP, :] = decay * c_mem + lax.dot_general(
                    vw, k, (((0,), (0,)), ((), ())), preferred_element_type=f32)
                n_scr[h:h + 1, :] = decay * n_row + jnp.sum(k.astype(f32) * w_col, axis=0, keepdims=True)
                m_scr[h:h + 1, :] = jnp.broadcast_to(m_new, (1, LANES))
            return 0

        lax.fori_loop(0, nc_tot, chunk, 0)

    def head_out(hs, o):
        outs = []
        for h in range(ML_HEADS):
            x = hs[:, h * ML_HP:(h + 1) * ML_HP]
            ms = jnp.sum(x * x, axis=1, keepdims=True) * (1.0 / ML_HEAD_DIM)
            outs.append(x * lax.rsqrt(ms + NORM_EPS))
        hn = jnp.concatenate(outs, axis=1) * ng_ref[...]
        return (hn * jax.nn.sigmoid(o.astype(f32))).astype(bf)

    outc_ref[0] = head_out(h_scr[0:ctx_len, :], oc_ref[0])

    def fin(c, _):
        r = pl.multiple_of(c * tc, tc)
        outl_ref[0, pl.ds(r, tc), :] = head_out(h_scr[pl.ds(ctx_len + r, tc), :], ol_ref[0, pl.ds(r, tc), :])
        return 0

    lax.fori_loop(0, nc_lat, fin, 0)


def mlstm_call(qk_c, qk_l, v_c, v_l, g_c, g_l, o_c, o_l, conv_w, conv_b, gate_b, norm_g, *, row_len):
    bsz, ctx_len, _ = qk_c.shape
    seq_len = qk_l.shape[1]
    tot = ctx_len + seq_len
    per_b = lambda a: pl.BlockSpec((1,) + a.shape[1:], lambda b: (b, 0, 0))
    full = lambda a: pl.BlockSpec(a.shape, lambda b: (0, 0))
    return pl.pallas_call(
        functools.partial(_mlstm_kernel, ctx_len=ctx_len, seq_len=seq_len, row_len=row_len),
        grid=(bsz,),
        in_specs=[per_b(a) for a in (qk_c, qk_l, v_c, v_l, g_c, g_l, o_c, o_l)]
        + [full(a) for a in (conv_w, conv_b, gate_b, norm_g)],
        out_specs=[pl.BlockSpec((1, ctx_len, ML_DP), lambda b: (b, 0, 0)),
                   pl.BlockSpec((1, seq_len, ML_DP), lambda b: (b, 0, 0))],
        out_shape=[jax.ShapeDtypeStruct((bsz, ctx_len, ML_DP), jnp.bfloat16),
                   jax.ShapeDtypeStruct((bsz, seq_len, ML_DP), jnp.bfloat16)],
        scratch_shapes=[pltpu.VMEM((tot, ML_DP), jnp.bfloat16)] * 3
        + [pltpu.VMEM((tot, GATE_PAD), jnp.float32), pltpu.VMEM((tot, ML_DP), jnp.float32),
           pltpu.VMEM((ML_DP, ML_HP), jnp.float32), pltpu.VMEM((8, LANES), jnp.float32),
           pltpu.VMEM((8, LANES), jnp.float32)],
        compiler_params=pltpu.CompilerParams(
            dimension_semantics=("parallel",),
            vmem_limit_bytes=VMEM_LIMIT_BYTES),
        name="mlstm",
    )(qk_c, qk_l, v_c, v_l, g_c, g_l, o_c, o_l, conv_w, conv_b, gate_b, norm_g)


def _hyena_kernel(cw_ref, hb_ref, v_ref, x1_ref, x2_ref, fr_ref, o_ref, g_scr, *, ch, na, nb, row_len):
    bf, f32 = jnp.bfloat16, jnp.float32
    nl = na * nb
    cblk = pl.program_id(0)
    row = lax.broadcasted_iota(jnp.int32, (ch, 1), 0) % row_len
    first, last = row == 0, row == row_len - 1
    lane = lax.broadcasted_iota(jnp.int32, (1, nl), 1)

    def short_conv(x_ref, j, part):
        x = x_ref[j].astype(f32)
        cidx = part * D_HY + cblk * HY_CB + j
        prev = jnp.where(first, 0.0, pltpu.roll(x, 1, 0))
        nxt = jnp.where(last, 0.0, pltpu.roll(x, ch - 1, 0))
        return cw_ref[3, cidx] + cw_ref[0, cidx] * prev + cw_ref[1, cidx] * x + cw_ref[2, cidx] * nxt

    def channel(j, _):
        z = short_conv(v_ref, j, 0)
        for o, xg_ref in ((0, x1_ref), (1, x2_ref)):
            gate = short_conv(xg_ref, j, 1 + o)
            for q in range(2 * na - 1):
                win = fr_ref[o, pl.ds(j, 1), q * ch:(q + 2) * ch]
                g = pltpu.roll(jnp.broadcast_to(win, (ch, 2 * ch)), ch, 1, stride=1, stride_axis=0)
                g_scr[q * ch:(q + 1) * ch, :] = g[:, 0:ch].astype(bf)
            zp = jnp.dot(g_scr[...], z.astype(bf), preferred_element_type=f32)
            y = zp[(na - 1) * ch:na * ch]
            for p in range(1, na):
                zpos = zp[(na - 1 - p) * ch:(na - p) * ch]
                zneg = zp[(2 * na - 1 - p) * ch:(2 * na - p) * ch]
                y = y + pltpu.roll(jnp.where(lane < (na - p) * nb, zpos, zneg), p * nb, 1)
            z = gate * (y + hb_ref[o, cblk * HY_CB + j] * z)
        o_ref[j] = z.astype(bf)
        return 0

    lax.fori_loop(0, HY_CB, channel, 0)


def hyena_call(conv_wb, hy_bias, ht, fr, *, ch, na, nb, row_len):
    nl = na * nb
    n_blk = D_HY // HY_CB
    part = lambda k: pl.BlockSpec((HY_CB, ch, nl), lambda i, *_: (i + k * n_blk, 0, 0))
    grid_spec = pltpu.PrefetchScalarGridSpec(
        num_scalar_prefetch=2,
        grid=(n_blk,),
        in_specs=[part(0), part(1), part(2),
                  pl.BlockSpec((HY_ORDER, HY_CB, 2 * ch * na), lambda i, *_: (0, i, 0))],
        out_specs=pl.BlockSpec((HY_CB, ch, nl), lambda i, *_: (i, 0, 0)),
        scratch_shapes=[pltpu.VMEM(((2 * na - 1) * ch, ch), jnp.bfloat16)],
    )
    return pl.pallas_call(
        functools.partial(_hyena_kernel, ch=ch, na=na, nb=nb, row_len=row_len),
        grid_spec=grid_spec,
        out_shape=jax.ShapeDtypeStruct((D_HY, ch, nl), jnp.bfloat16),
        compiler_params=pltpu.CompilerParams(
            dimension_semantics=("parallel",),
            vmem_limit_bytes=VMEM_LIMIT_BYTES),
        name="hyena",
    )(conv_wb, hy_bias, ht, ht, ht, fr)


def hyena_filters(seq_len, w1, b1, w2, b2, w3, b3, sin_freq):
    t = jnp.linspace(0.0, 1.0, seq_len, dtype=jnp.float32)[:, None]
    w = 2.0 * math.pi * jnp.arange(seq_len, dtype=jnp.float32)[:, None] / seq_len
    f = jnp.linspace(1e-4, HY_BANDS - 1, HY_BANDS, dtype=jnp.float32)[None, :]
    z = jnp.concatenate([t, jnp.cos(f * w), -jnp.sin(f * w)], axis=-1)
    hdn = jnp.sin(sin_freq[0] * (z @ w1 + b1))
    hdn = jnp.sin(sin_freq[1] * (hdn @ w2 + b2))
    hf = (hdn @ w3 + b3).astype(jnp.float32).reshape(seq_len, HY_ORDER, 2, D_HY)
    deltas = jnp.abs(jnp.linspace(HY_MIN_DECAY, HY_MAX_DECAY, D_HY, dtype=jnp.float32))
    decay = jnp.exp(-t * deltas)
    return hf * decay[:, None, None, :]


def hyena_reversed_taps(seq_len, filter_params):
    filt = hyena_filters(seq_len, *filter_params)
    out = []
    for o in range(HY_ORDER):
        h_fwd, h_bwd = filt[:, o, 0], filt[:, o, 1]
        k = jnp.concatenate([h_fwd, jnp.zeros_like(h_fwd[:1]), h_bwd[:0:-1]], axis=0)
        k = k / jnp.sum(jnp.abs(k), axis=0, keepdims=True)
        idx = (seq_len - jnp.arange(2 * seq_len)) % (2 * seq_len)
        out.append(k[idx].T)
    return jnp.stack(out)


def hyena_mixer(hy, conv_wb, hy_bias, fr, *, ch, row_len, pad_batch):
    bsz, seq_len, width = hy.shape
    na = seq_len // ch
    nb = pad_batch
    ht = hy.reshape(bsz, na, ch, width).transpose(3, 2, 1, 0)
    if nb != bsz:
        ht = jnp.pad(ht, ((0, 0), (0, 0), (0, 0), (0, nb - bsz)))
    out = hyena_call(conv_wb, hy_bias, ht.reshape(width, ch, na * nb), fr, ch=ch, na=na, nb=nb, row_len=row_len)
    out = out.reshape(D_HY, ch, na, nb)[..., :bsz]
    return out.transpose(3, 2, 1, 0).reshape(bsz, seq_len, D_HY)


def moe(h2, logits, w1g, w1l, b1g, b1l, w2, b2):
    n_tok, d = h2.shape
    top_val, top_idx = lax.top_k(logits, TOP_K)
    weights = jax.nn.softmax(top_val, axis=-1)
    flat_e = top_idx.reshape(-1).astype(jnp.int32)
    n_assign = flat_e.shape[0]
    n_blocks = -(-n_assign // MOE_ROWS) + N_EXPERTS
    n_slots = n_blocks * MOE_ROWS
    iota = jnp.arange(n_assign, dtype=jnp.int32)
    e_sorted, order = lax.sort((flat_e, iota), num_keys=1, is_stable=True)
    counts = jnp.sum(flat_e[:, None] == jnp.arange(N_EXPERTS, dtype=jnp.int32)[None, :], axis=0).astype(jnp.int32)
    padded = (counts + MOE_ROWS - 1) // MOE_ROWS * MOE_ROWS
    end_pad = jnp.cumsum(padded)
    start_pad = end_pad - padded
    start = jnp.cumsum(counts) - counts
    slot = jnp.arange(n_slots, dtype=jnp.int32)
    slot_e = jnp.minimum(jnp.searchsorted(end_pad, slot, side='right'), N_EXPERTS - 1).astype(jnp.int32)
    off = slot - start_pad[slot_e]
    src = jnp.where(off < counts[slot_e], start[slot_e] + off, 0)
    slot_tok = order[src] // TOP_K
    dest = start_pad[e_sorted] + iota - start[e_sorted]
    _, pos = lax.sort((order, dest), num_keys=1)
    block_e = slot_e[::MOE_ROWS]
    n_used = (end_pad[-1] // MOE_ROWS).astype(jnp.int32).reshape(1)
    out = moe_expert_blocks(block_e, n_used, h2[slot_tok], w1g, w1l, b1g, b1l, w2, b2)
    picked = out[pos].reshape(n_tok, TOP_K, d).astype(jnp.float32)
    return jnp.sum(picked * weights[:, :, None], axis=1)


def rmsnorm(x, g):
    y = x * lax.rsqrt(jnp.mean(x * x, axis=-1, keepdims=True) + NORM_EPS)
    return y * g


def kernel(x, c, ctx, c_ctx, ada_w, ada_b, norm1_g, norm2_g, final_norm_g, w_in,
           hy_conv_w, hy_conv_b, hy_f_w1, hy_f_b1, hy_f_w2, hy_f_b2, hy_f_w3, hy_f_b3, hy_sin_freq, hy_bias,
           s5_lam_re, s5_lam_im, s5_log_dt, s5_b_re, s5_b_im, s5_c_re, s5_c_im, s5_d, s5_glu_w,
           ml_conv_w, ml_conv_b, ml_gate_b, ml_norm_g,
           w_br_hy, w_br_s5, w_br_ml, w_out,
           moe_router_w, moe_router_b, moe_w1, moe_b1, moe_w2, moe_b2):
    bsz, seq_len, d = x.shape
    ctx_len = ctx.shape[1]
    bf, f32 = jnp.bfloat16, jnp.float32
    silu_c = jax.nn.silu(c.astype(f32))
    silu_cc = jax.nn.silu(c_ctx.astype(f32))[None]
    x2 = x.reshape(bsz * seq_len, d)
    c2 = ctx.reshape(bsz * ctx_len, d)
    for l in range(DEPTH):
        need_ctx = l < DEPTH - 1
        mod_l = (silu_c @ ada_w[l] + ada_b[l])[:, None, :]
        mod_c = (silu_cc @ ada_w[l] + ada_b[l])[:, None, :]
        sh1_l, sc1_l, g1_l, sh2_l, sc2_l, g2_l = jnp.split(mod_l, 6, axis=-1)
        sh1_c, sc1_c, g1_c, sh2_c, sc2_c, g2_c = jnp.split(mod_c, 6, axis=-1)

        w_in_p = permute_w_in(w_in[l])
        gain1 = norm1_g[l].reshape(1, d)
        s5_l, qk_l, v_l, hy_l, o_l, mg_l, gt_l = norm_mod_project(x2, sh1_l, sc1_l, gain1, w_in_p, seq_len)
        s5_c, qk_c, v_c, hy_c, o_c, mg_c, gt_c = norm_mod_project(c2, sh1_c, sc1_c, gain1, w_in_p, bsz * ctx_len)
        per_b = lambda a, n: a.reshape(bsz, n, a.shape[-1])

        hy_params = (hy_f_w1[l], hy_f_b1[l], hy_f_w2[l], hy_f_b2[l], hy_f_w3[l], hy_f_b3[l], hy_sin_freq[l])
        conv_wb = jnp.concatenate([hy_conv_w[l], hy_conv_b[l][None]], axis=0)
        hy_out_l = hyena_mixer(per_b(hy_l, seq_len), conv_wb, hy_bias[l], hyena_reversed_taps(seq_len, hy_params),
                               ch=HY_CH, row_len=GRID_W, pad_batch=bsz)
        s5_w = s5_chunk_weights(s5_lam_re[l], s5_lam_im[l], s5_log_dt[l], s5_b_re[l], s5_b_im[l],
                                s5_c_re[l], s5_c_im[l], s5_d[l])
        s5_out_l, s5_out_c = s5_scan(per_b(s5_l, seq_len), per_b(s5_c, ctx_len), s5_w)
        ml_cw = jnp.concatenate([pad_heads(ml_conv_w[l][:, :D_ML]), pad_heads(ml_conv_w[l][:, D_ML:])], axis=1)
        ml_cb = jnp.concatenate([pad_heads(ml_conv_b[l][:D_ML]), pad_heads(ml_conv_b[l][D_ML:])])[None]
        ml_gb = jnp.pad(ml_gate_b[l], (0, GATE_PAD - 4 * ML_HEADS))[None]
        ml_out_c, ml_out_l = mlstm_call(
            per_b(qk_c, ctx_len), per_b(qk_l, seq_len), per_b(v_c, ctx_len), per_b(v_l, seq_len),
            per_b(gt_c, ctx_len), per_b(gt_l, seq_len), per_b(o_c, ctx_len), per_b(o_l, seq_len),
            ml_cw, ml_cb, ml_gb, pad_heads(ml_norm_g[l])[None], row_len=GRID_W)

        wglu, wh, ws, wo = s5_glu_w[l].astype(bf), w_br_hy[l].astype(bf), w_br_s5[l].astype(bf), w_out[l].astype(bf)
        wm = pad_heads(w_br_ml[l].T).T.astype(bf)
        rw = jnp.pad(moe_router_w[l], ((0, 0), (0, ROUTER_PAD - N_EXPERTS))).astype(bf)
        rb = jnp.pad(moe_router_b[l], (0, ROUTER_PAD - N_EXPERTS))[None]
        n2 = norm2_g[l].reshape(1, d)
        flat = lambda a: a.reshape(-1, a.shape[-1])
        x2, h2_l, lg_l = merge_project_residual(
            flat(hy_out_l), flat(s5_out_l), flat(ml_out_l), mg_l, x2, g1_l, sh2_l, sc2_l, n2,
            wglu, wh, ws, wm, wo, rw, rb, seq_len)

        w1 = moe_w1[l]
        moe_w = (w1[:, :, 0::2].astype(bf), w1[:, :, 1::2].astype(bf),
                 moe_b1[l][:, None, 0::2], moe_b1[l][:, None, 1::2],
                 moe_w2[l].astype(bf), moe_b2[l][:, None, :])
        if need_ctx:
            hy_out_c = hyena_mixer(per_b(hy_c, ctx_len), conv_wb, hy_bias[l],
                                   hyena_reversed_taps(ctx_len, hy_params),
                                   ch=ctx_len, row_len=ctx_len, pad_batch=LANES)
            c2, h2_c, lg_c = merge_project_residual(
                flat(hy_out_c), flat(s5_out_c), flat(ml_out_c), mg_c, c2, g1_c, sh2_c, sc2_c, n2,
                wglu, wh, ws, wm, wo, rw, rb, bsz * ctx_len)
            tok = jnp.concatenate([h2_c.reshape(bsz, ctx_len, d), h2_l.reshape(bsz, seq_len, d)], axis=1)
            lg = jnp.concatenate([lg_c.reshape(bsz, ctx_len, -1), lg_l.reshape(bsz, seq_len, -1)], axis=1)
            f = moe(tok.reshape(-1, d), lg.reshape(-1, ROUTER_PAD)[:, :N_EXPERTS], *moe_w)
            f = f.reshape(bsz, ctx_len + seq_len, d)
            c2 = c2 + (g2_c * f[:, :ctx_len]).reshape(-1, d)
            x2 = x2 + (g2_l * f[:, ctx_len:]).reshape(-1, d)
        else:
            f = moe(h2_l, lg_l[:, :N_EXPERTS], *moe_w)
            x2 = x2 + (g2_l * f.reshape(bsz, seq_len, d)).reshape(-1, d)
    return rmsnorm(x2, final_norm_g).reshape(bsz, seq_len, d)
```

```python
import functools
import math

import jax
import jax.numpy as jnp
from jax import lax
from jax.experimental import pallas as pl
from jax.experimental.pallas import tpu as pltpu

D_MODEL = 1024
DEPTH = 4
GRID_W = 64

D_HY = 384
D_S5 = 384
D_ML = 384
N_BRANCH = 3
SHORT_CONV = 3

HY_ORDER = 2
HY_EMB = 33
HY_BANDS = (HY_EMB - 1) // 2
HY_DECAY_TARGET = 1e-2
HY_FAST_DECAY_PCT = 0.3
HY_SLOW_DECAY_PCT = 1.5
HY_MIN_DECAY = math.log(HY_DECAY_TARGET) / HY_SLOW_DECAY_PCT
HY_MAX_DECAY = math.log(HY_DECAY_TARGET) / HY_FAST_DECAY_PCT

S5_GROUP = 16
S5_GROUPS = D_S5 // S5_GROUP
S5_STATE = 64

ML_HEADS = 4
ML_HEAD_DIM = D_ML // ML_HEADS
NEG = -1e30

N_EXPERTS = 32
TOP_K = 4
D_FF_EXPERT = 512
SWIGLU_LIMIT = 7.0
SWIGLU_ALPHA = 1.702

NORM_EPS = 1e-6

COL_SIZES = (D_S5, 2 * D_ML, D_ML, 4 * ML_HEADS, (1 + HY_ORDER) * D_HY, D_ML, N_BRANCH * D_MODEL)

LANES = 128
VMEM_LIMIT_BYTES = 56 * 1024 * 1024

ML_HP = LANES
ML_DP = ML_HEADS * ML_HP
ML_TC = 128
GATE_PAD = LANES

PROJ_OUT = (("s5", D_S5, jnp.bfloat16), ("qk", 2 * ML_DP, jnp.bfloat16), ("v", ML_DP, jnp.bfloat16),
            ("hy", (1 + HY_ORDER) * D_HY, jnp.bfloat16), ("o", ML_DP, jnp.bfloat16),
            ("mg", N_BRANCH * D_MODEL, jnp.bfloat16), ("gt", GATE_PAD, jnp.float32))
PROJ_COLS = sum(w for _, w, _ in PROJ_OUT)
PROJ_TM = 512

MOE_ROWS = 512
MERGE_TM = 512
ROUTER_PAD = LANES

HY_CB = 8
HY_CH = 128


def _proj_kernel(x_ref, shift_ref, scale_ref, g_ref, w_ref, *o_refs):
    x = x_ref[...]
    ms = jnp.mean(x * x, axis=-1, keepdims=True)
    y = x * lax.rsqrt(ms + NORM_EPS) * g_ref[...]
    h = (y * (1.0 + scale_ref[0]) + shift_ref[0]).astype(jnp.bfloat16)
    start = 0
    for o_ref, (_, width, _) in zip(o_refs, PROJ_OUT):
        o_ref[...] = jnp.dot(h, w_ref[:, start:start + width],
                             preferred_element_type=jnp.float32).astype(o_ref.dtype)
        start += width


def norm_mod_project(x2d, shift, scale, gain, w_bf16, rows_per_mod):
    rows, d = x2d.shape
    tm = PROJ_TM
    blocks_per_mod = rows_per_mod // tm
    return pl.pallas_call(
        _proj_kernel,
        grid=(rows // tm,),
        in_specs=[
            pl.BlockSpec((tm, d), lambda i: (i, 0)),
            pl.BlockSpec((1, 1, d), lambda i: (i // blocks_per_mod, 0, 0)),
            pl.BlockSpec((1, 1, d), lambda i: (i // blocks_per_mod, 0, 0)),
            pl.BlockSpec((1, d), lambda i: (0, 0)),
            pl.BlockSpec((d, PROJ_COLS), lambda i: (0, 0)),
        ],
        out_specs=[pl.BlockSpec((tm, w), lambda i: (i, 0)) for _, w, _ in PROJ_OUT],
        out_shape=[jax.ShapeDtypeStruct((rows, w), dt) for _, w, dt in PROJ_OUT],
        compiler_params=pltpu.CompilerParams(
            dimension_semantics=("parallel",),
            vmem_limit_bytes=VMEM_LIMIT_BYTES),
        name="norm_mod_project",
    )(x2d, shift, scale, gain, w_bf16)


def pad_heads(w):
    lead = w.shape[:-1]
    w = w.reshape(lead + (ML_HEADS, ML_HEAD_DIM))
    w = jnp.pad(w, [(0, 0)] * len(lead) + [(0, 0), (0, ML_HP - ML_HEAD_DIM)])
    return w.reshape(lead + (ML_DP,))


def permute_w_in(w_in):
    cols, start = [], 0
    for s in COL_SIZES:
        cols.append(w_in[:, start:start + s])
        start += s
    s5, qk, v, gt, hy, o, mg = cols
    qk = jnp.concatenate([pad_heads(qk[:, :D_ML]), pad_heads(qk[:, D_ML:])], axis=1)
    gt = jnp.pad(gt, ((0, 0), (0, GATE_PAD - gt.shape[1])))
    return jnp.concatenate([s5, qk, pad_heads(v), hy, pad_heads(o), mg, gt], axis=1).astype(jnp.bfloat16)


def _merge_kernel(hy_ref, s5_ref, ml_ref, mg_ref, x_ref, g1_ref, sh2_ref, sc2_ref, n2_ref,
                  wglu_ref, wh_ref, ws_ref, wm_ref, wo_ref, rw_ref, rb_ref,
                  xo_ref, h2_ref, lg_ref):
    bf = jnp.bfloat16
    f32 = jnp.float32
    d = D_MODEL
    g = jax.nn.gelu(s5_ref[...].astype(f32))
    s5 = g * jax.nn.sigmoid(jnp.dot(g.astype(bf), wglu_ref[...], preferred_element_type=f32))
    y = jax.nn.sigmoid(mg_ref[:, 0:d].astype(f32)) * jnp.dot(hy_ref[...], wh_ref[...], preferred_element_type=f32)
    y = y + jax.nn.sigmoid(mg_ref[:, d:2 * d].astype(f32)) * jnp.dot(s5.astype(bf), ws_ref[...],
                                                                     preferred_element_type=f32)
    y = y + jax.nn.sigmoid(mg_ref[:, 2 * d:3 * d].astype(f32)) * jnp.dot(ml_ref[...], wm_ref[...],
                                                                         preferred_element_type=f32)
    out = jnp.dot(y.astype(bf), wo_ref[...], preferred_element_type=f32)
    xn = x_ref[...] + g1_ref[0] * out
    xo_ref[...] = xn
    ms = jnp.mean(xn * xn, axis=-1, keepdims=True)
    h2 = (xn * lax.rsqrt(ms + NORM_EPS) * n2_ref[...]) * (1.0 + sc2_ref[0]) + sh2_ref[0]
    h2b = h2.astype(bf)
    h2_ref[...] = h2b
    lg_ref[...] = jnp.dot(h2b, rw_ref[...], preferred_element_type=f32) + rb_ref[...]


def merge_project_residual(hy, s5, ml, mg, x2d, g1, sh2, sc2, n2, wglu, wh, ws, wm, wo, rw, rb, rows_per_mod):
    rows, d = x2d.shape
    tm = MERGE_TM
    blocks_per_mod = rows_per_mod // tm
    row_spec = lambda c: pl.BlockSpec((tm, c), lambda i: (i, 0))
    mod_spec = pl.BlockSpec((1, 1, d), lambda i: (i // blocks_per_mod, 0, 0))
    full = lambda a: pl.BlockSpec(a.shape, lambda i: (0, 0))
    return pl.pallas_call(
        _merge_kernel,
        grid=(rows // tm,),
        in_specs=[row_spec(D_HY), row_spec(D_S5), row_spec(ML_DP), row_spec(N_BRANCH * d), row_spec(d),
                  mod_spec, mod_spec, mod_spec, full(n2),
                  full(wglu), full(wh), full(ws), full(wm), full(wo), full(rw), full(rb)],
        out_specs=[row_spec(d), row_spec(d), row_spec(ROUTER_PAD)],
        out_shape=[jax.ShapeDtypeStruct((rows, d), jnp.float32),
                   jax.ShapeDtypeStruct((rows, d), jnp.bfloat16),
                   jax.ShapeDtypeStruct((rows, ROUTER_PAD), jnp.float32)],
        compiler_params=pltpu.CompilerParams(
            dimension_semantics=("parallel",),
            vmem_limit_bytes=VMEM_LIMIT_BYTES),
        name="merge_project_residual",
    )(hy, s5, ml, mg, x2d, g1, sh2, sc2, n2, wglu, wh, ws, wm, wo, rw, rb)


def _deinterleave_kernel(w_ref, p_ref, o_ref):
    o_ref[...] = jnp.dot(w_ref[...].astype(jnp.bfloat16), p_ref[...],
                         preferred_element_type=jnp.float32).astype(jnp.bfloat16)


def deinterleave_cast(w2d):
    rows, n = w2d.shape
    tm = 1024
    src = jnp.concatenate([jnp.arange(0, n, 2), jnp.arange(1, n, 2)])
    perm = (jnp.arange(n)[:, None] == src[None, :]).astype(jnp.bfloat16)
    return pl.pallas_call(
        _deinterleave_kernel,
        grid=(rows // tm,),
        in_specs=[pl.BlockSpec((tm, n), lambda i: (i, 0)), pl.BlockSpec((n, n), lambda i: (0, 0))],
        out_specs=pl.BlockSpec((tm, n), lambda i: (i, 0)),
        out_shape=jax.ShapeDtypeStruct((rows, n), jnp.bfloat16),
        compiler_params=pltpu.CompilerParams(
            dimension_semantics=("parallel",),
            vmem_limit_bytes=VMEM_LIMIT_BYTES),
        name="deinterleave_cast",
    )(w2d, perm)


def _moe_kernel(be_ref, nu_ref, x_ref, w1_ref, b1g_ref, b1l_ref, w2_ref, b2_ref, o_ref):
    f = D_FF_EXPERT

    @pl.when(pl.program_id(0) < nu_ref[0])
    def _():
        x = x_ref[...]
        hg = jnp.dot(x, w1_ref[0, :, 0:f], preferred_element_type=jnp.float32) + b1g_ref[0]
        hl = jnp.dot(x, w1_ref[0, :, f:2 * f], preferred_element_type=jnp.float32) + b1l_ref[0]
        x_glu = jnp.minimum(hg, SWIGLU_LIMIT)
        x_lin = jnp.clip(hl, -SWIGLU_LIMIT, SWIGLU_LIMIT)
        act = x_glu * jax.nn.sigmoid(SWIGLU_ALPHA * x_glu) * (x_lin + 1.0)
        out = jnp.dot(act.astype(jnp.bfloat16), w2_ref[0], preferred_element_type=jnp.float32) + b2_ref[0]
        o_ref[...] = out.astype(o_ref.dtype)


def moe_expert_blocks(block_e, n_used, x_sorted, w1p, b1g, b1l, w2, b2):
    n_slots, d = x_sorted.shape
    n_blocks = n_slots // MOE_ROWS
    f = D_FF_EXPERT

    def row_map(i, be, nu):
        return (jnp.minimum(i, nu[0] - 1), 0)

    def w_map(i, be, nu):
        return (be[jnp.minimum(i, nu[0] - 1)], 0, 0)

    grid_spec = pltpu.PrefetchScalarGridSpec(
        num_scalar_prefetch=2,
        grid=(n_blocks,),
        in_specs=[
            pl.BlockSpec((MOE_ROWS, d), row_map),
            pl.BlockSpec((1, d, 2 * f), w_map),
            pl.BlockSpec((1, 1, f), w_map),
            pl.BlockSpec((1, 1, f), w_map),
            pl.BlockSpec((1, f, d), w_map),
            pl.BlockSpec((1, 1, d), w_map),
        ],
        out_specs=pl.BlockSpec((MOE_ROWS, d), row_map),
    )
    return pl.pallas_call(
        _moe_kernel,
        grid_spec=grid_spec,
        out_shape=jax.ShapeDtypeStruct((n_slots, d), jnp.bfloat16),
        compiler_params=pltpu.CompilerParams(
            dimension_semantics=("arbitrary",),
            vmem_limit_bytes=VMEM_LIMIT_BYTES),
        name="moe_expert_blocks",
    )(block_e, n_used, x_sorted, w1p, b1g, b1l, w2, b2)


S5_T = 16
S5_W = S5_T * S5_GROUP
S5_HALF = 2 * S5_STATE


def _s5_kernel(u_ref, wp_ref, wm_ref, wq0_ref, wq1_ref, ar_ref, ai_ref, y_ref, s_scr, x_scr, yb_scr, ur_scr,
               *, n_batch, ctx_chunks, lat_chunks):
    nb = n_batch
    u = u_ref[0]
    segments = ((0, ctx_chunks), (ctx_chunks * nb, lat_chunks))
    for seg_start, seg_chunks in segments:
        for i in range(seg_chunks):
            dst = seg_start + nb * i
            src = seg_start + nb * (seg_chunks - 1 - i)
            ur_scr[dst:dst + nb, :] = u_ref[0, src:src + nb, :]
    s_scr[...] = (jnp.dot(u, wp_ref[0, 0:S5_W, :], preferred_element_type=jnp.float32)
                  + jnp.dot(ur_scr[...], wp_ref[0, S5_W:2 * S5_W, :], preferred_element_type=jnp.float32))
    ar = jnp.broadcast_to(ar_ref[0], (nb, S5_HALF))
    ai = jnp.broadcast_to(ai_ref[0], (nb, S5_HALF))

    def step(i, carry):
        xr, xi = carry
        r = pl.multiple_of(i * nb, nb)
        x_scr[pl.ds(r, nb), 0:S5_HALF] = xr
        x_scr[pl.ds(r, nb), S5_HALF:2 * S5_HALF] = xi
        sr = s_scr[pl.ds(r, nb), 0:S5_HALF]
        si = s_scr[pl.ds(r, nb), S5_HALF:2 * S5_HALF]
        return ar * xr - ai * xi + sr, ar * xi + ai * xr + si

    zero = jnp.zeros((nb, S5_HALF), jnp.float32)
    lax.fori_loop(0, ctx_chunks + lat_chunks, step, (zero, zero))

    xin = x_scr[...].astype(jnp.bfloat16)
    s_scr[...] = (jnp.dot(u, wm_ref[0], preferred_element_type=jnp.float32)
                  + jnp.dot(xin, wq0_ref[0], preferred_element_type=jnp.float32))
    yb_scr[...] = jnp.dot(xin, wq1_ref[0], preferred_element_type=jnp.float32)
    for seg_start, seg_chunks in segments:
        for i in range(seg_chunks):
            dst = seg_start + nb * i
            src = seg_start + nb * (seg_chunks - 1 - i)
            y_ref[0, dst:dst + nb, :] = (s_scr[dst:dst + nb, :] + yb_scr[src:src + nb, :]).astype(y_ref.dtype)


def s5_scan_call(u, wp, wm, wq0, wq1, ar, ai, *, n_batch, ctx_chunks):
    groups, rows, _ = u.shape
    lat_chunks = rows // n_batch - ctx_chunks
    per_group = lambda a: pl.BlockSpec((1,) + a.shape[1:], lambda g: (g, 0, 0))
    return pl.pallas_call(
        functools.partial(_s5_kernel, n_batch=n_batch, ctx_chunks=ctx_chunks, lat_chunks=lat_chunks),
        grid=(groups,),
        in_specs=[per_group(a) for a in (u, wp, wm, wq0, wq1, ar, ai)],
        out_specs=pl.BlockSpec((1, rows, S5_W), lambda g: (g, 0, 0)),
        out_shape=jax.ShapeDtypeStruct((groups, rows, S5_W), jnp.bfloat16),
        scratch_shapes=[pltpu.VMEM((rows, S5_W), jnp.float32)] * 3 + [pltpu.VMEM((rows, S5_W), jnp.bfloat16)],
        compiler_params=pltpu.CompilerParams(
            dimension_semantics=("parallel",),
            vmem_limit_bytes=VMEM_LIMIT_BYTES),
        name="s5_scan",
    )(u, wp, wm, wq0, wq1, ar, ai)


def s5_chunk_weights(lam_re, lam_im, log_dt, b_re, b_im, c_re, c_im, d_skip):
    f32 = jnp.float32
    t_len, g_n, p_n, n_n = S5_T, S5_GROUPS, S5_STATE, S5_GROUP
    b_mat = lax.complex(b_re.astype(f32), b_im.astype(f32))
    c_mat = lax.complex(c_re.astype(f32), c_im.astype(f32))
    lam = lax.complex(lam_re.astype(f32), lam_im.astype(f32))
    lam_dt = lam * jnp.exp(log_dt.astype(f32))[..., None]
    b_bar = ((jnp.exp(lam_dt) - 1.0) / lam)[..., None] * b_mat
    j = jnp.arange(t_len + 1, dtype=f32)
    pw = jnp.exp(j[:, None, None, None] * lam_dt[None])

    kern = [jnp.real(jnp.einsum('gnp,jgp,gpm->jgnm', c_mat, pw[:t_len, d], b_bar[d])) for d in range(2)]
    idx = jnp.arange(t_len)
    diff = idx[None, :] - idx[:, None]
    k0 = kern[0][jnp.clip(diff, 0, None)]
    k1 = kern[1][jnp.clip(-diff, 0, None)]
    m5 = (jnp.where((diff >= 0)[:, :, None, None, None], k0, 0.0)
          + jnp.where((diff <= 0)[:, :, None, None, None], k1, 0.0))
    skip = jnp.eye(t_len, dtype=f32)[:, :, None, None, None] * (
        d_skip.astype(f32).reshape(g_n, n_n)[None, None, :, :, None] * jnp.eye(n_n, dtype=f32)[None, None, None])
    wm = (m5 + skip).transpose(2, 0, 4, 1, 3).reshape(g_n, S5_W, S5_W)

    pf = jnp.einsum('sgp,gpm->gsmp', pw[t_len - 1 - idx, 0], b_bar[0]).reshape(g_n, S5_W, p_n)
    pb = jnp.einsum('sgp,gpm->gsmp', pw[idx, 1], b_bar[1]).reshape(g_n, S5_W, p_n)
    z = jnp.zeros_like(jnp.real(pf))
    wp = jnp.concatenate([
        jnp.concatenate([jnp.real(pf), z, jnp.imag(pf), z], axis=-1),
        jnp.concatenate([z, jnp.real(pb), z, jnp.imag(pb)], axis=-1)], axis=1)

    q0 = jnp.einsum('gnp,tgp->gptn', c_mat, pw[idx + 1, 0]).reshape(g_n, p_n, S5_W)
    q1 = jnp.einsum('gnp,tgp->gptn', c_mat, pw[t_len - idx, 1]).reshape(g_n, p_n, S5_W)
    zq = jnp.zeros_like(jnp.real(q0))
    wq0 = jnp.concatenate([jnp.real(q0), zq, -jnp.imag(q0), zq], axis=1)
    wq1 = jnp.concatenate([zq, jnp.real(q1), zq, -jnp.imag(q1)], axis=1)
    lam_t = pw[t_len]
    ar = jnp.concatenate([jnp.real(lam_t[0]), jnp.real(lam_t[1])], axis=-1)[:, None, :]
    ai = jnp.concatenate([jnp.imag(lam_t[0]), jnp.imag(lam_t[1])], axis=-1)[:, None, :]
    bf = jnp.bfloat16
    return wp.astype(bf), wm.astype(bf), wq0.astype(bf), wq1.astype(bf), ar, ai


def s5_scan(u_l, u_c, weights):
    bsz, seq_len, _ = u_l.shape
    ctx_len = u_c.shape[1]

    def to_chunks(a):
        a = a.reshape(bsz, a.shape[1] // S5_T, S5_T, S5_GROUPS, S5_GROUP).transpose(3, 1, 0, 2, 4)
        return a.reshape(S5_GROUPS, -1, S5_W)

    u = jnp.concatenate([to_chunks(u_c), to_chunks(u_l)], axis=1).astype(jnp.bfloat16)
    y = s5_scan_call(u, *weights, n_batch=bsz, ctx_chunks=ctx_len // S5_T)

    def from_chunks(a, length):
        a = a.reshape(S5_GROUPS, length // S5_T, bsz, S5_T, S5_GROUP)
        return a.transpose(2, 1, 3, 0, 4).reshape(bsz, length, D_S5)

    ctx_rows = ctx_len // S5_T * bsz
    return from_chunks(y[:, ctx_rows:], seq_len), from_chunks(y[:, :ctx_rows], ctx_len)


def _dot01(a01, x):
    bf, f32 = jnp.bfloat16, jnp.float32
    hi = x.astype(bf)
    r1 = x - hi.astype(f32)
    mid = r1.astype(bf)
    lo = (r1 - mid.astype(f32)).astype(bf)
    d = lambda y: jnp.dot(a01, y, preferred_element_type=f32)
    return d(hi) + d(mid) + d(lo)


def _mlstm_kernel(qkc_ref, qkl_ref, vc_ref, vl_ref, gc_ref, gl_ref, oc_ref, ol_ref,
                  cw_ref, cb_ref, gb_ref, ng_ref,
                  outc_ref, outl_ref,
                  q_scr, k_scr, v_scr, g_scr, h_scr, c_scr, n_scr, m_scr,
                  *, ctx_len, seq_len, row_len):
    bf, f32 = jnp.bfloat16, jnp.float32
    tc = ML_TC
    nc_ctx = ctx_len // tc
    nc_lat = seq_len // tc
    nc_tot = nc_ctx + nc_lat

    def conv_silu(x, n_rows, rlen):
        row = lax.broadcasted_iota(jnp.int32, (n_rows, 1), 0) % rlen
        prev = jnp.where(row == 0, 0.0, pltpu.roll(x, 1, 0))
        nxt = jnp.where(row == rlen - 1, 0.0, pltpu.roll(x, n_rows - 1, 0))
        y = cb_ref[...] + prev * cw_ref[0:1, :] + x * cw_ref[1:2, :] + nxt * cw_ref[2:3, :]
        return y * jax.nn.sigmoid(y)

    kscale = ML_HEAD_DIM ** -0.5
    a = conv_silu(qkc_ref[0].astype(f32), ctx_len, ctx_len)
    q_scr[0:ctx_len, :] = a[:, 0:ML_DP].astype(bf)
    k_scr[0:ctx_len, :] = (a[:, ML_DP:2 * ML_DP] * kscale).astype(bf)
    v_scr[0:ctx_len, :] = vc_ref[0]
    g_scr[0:ctx_len, :] = gc_ref[0] + gb_ref[...]

    def prep(c, _):
        r = pl.multiple_of(c * tc, tc)
        a = conv_silu(qkl_ref[0, pl.ds(r, tc), :].astype(f32), tc, row_len)
        q_scr[pl.ds(ctx_len + r, tc), :] = a[:, 0:ML_DP].astype(bf)
        k_scr[pl.ds(ctx_len + r, tc), :] = (a[:, ML_DP:2 * ML_DP] * kscale).astype(bf)
        v_scr[pl.ds(ctx_len + r, tc), :] = vl_ref[0, pl.ds(r, tc), :]
        g_scr[pl.ds(ctx_len + r, tc), :] = gl_ref[0, pl.ds(r, tc), :] + gb_ref[...]
        return 0

    lax.fori_loop(0, nc_lat, prep, 0)

    ti = lax.broadcasted_iota(jnp.int32, (tc, tc), 0)
    si = lax.broadcasted_iota(jnp.int32, (tc, tc), 1)

    for d in range(2):
        causal = (si <= ti) if d == 0 else (si >= ti)
        tri = causal.astype(bf)
        c_scr[...] = jnp.zeros_like(c_scr)
        n_scr[...] = jnp.zeros_like(n_scr)
        m_scr[...] = jnp.full_like(m_scr, NEG)

        def chunk(i, _, d=d, causal=causal, tri=tri):
            if d == 0:
                ci = i
            else:
                ci = jnp.where(i < nc_ctx, nc_ctx - 1 - i, nc_tot + nc_ctx - 1 - i)
            r0 = pl.multiple_of(ci * tc, tc)
            gts = g_scr[pl.ds(r0, tc), :]
            logf = jax.nn.log_sigmoid(gts)
            bm = _dot01(tri, logf)
            gts_t = gts.T
            bm_t = bm.T
            b_end_row = bm[tc - 1:tc, :] if d == 0 else bm[0:1, :]
            for h in range(ML_HEADS):
                icol = h + 8 * d
                fcol = ML_HEADS + h + 8 * d
                lanes = slice(h * ML_HP, (h + 1) * ML_HP)
                i_col = gts[:, icol:icol + 1]
                b_col = bm[:, fcol:fcol + 1]
                i_row = gts_t[icol:icol + 1, :]
                b_row = bm_t[fcol:fcol + 1, :]
                b_end = b_end_row[:, fcol:fcol + 1]
                m_prev = m_scr[h:h + 1, 0:1]
                dmat = jnp.where(causal, b_col - b_row + i_row, NEG)
                a_col = b_col + m_prev
                m_out = jnp.maximum(a_col, jnp.max(dmat, axis=1, keepdims=True))
                wmat = jnp.exp(dmat - m_out)
                wa = jnp.exp(a_col - m_out)
                q = q_scr[pl.ds(r0, tc), lanes]
                k = k_scr[pl.ds(r0, tc), lanes]
                v = v_scr[pl.ds(r0, tc), lanes]
                s = lax.dot_general(q, k, (((1,), (1,)), ((), ())), preferred_element_type=f32) * wmat
                c_mem = c_scr[h * ML_HP:(h + 1) * ML_HP, :]
                n_row = n_scr[h:h + 1, :]
                qf = q.astype(f32)
                num = (jnp.dot(s.astype(bf), v, preferred_element_type=f32)
                       + wa * lax.dot_general(q, c_mem.astype(bf), (((1,), (1,)), ((), ())),
                                              preferred_element_type=f32))
                den = jnp.sum(s, axis=1, keepdims=True) + wa * jnp.sum(qf * n_row, axis=1, keepdims=True)
                hval = num / jnp.maximum(jnp.abs(den), jnp.exp(-m_out))
                if d == 0:
                    h_scr[pl.ds(r0, tc), lanes] = hval
                else:
                    h_scr[pl.ds(r0, tc), lanes] += hval
                g_col = b_end - b_col + i_col
                m_new = jnp.maximum(b_end + m_prev, jnp.max(g_col, axis=0, keepdims=True))
                decay = jnp.exp(b_end + m_prev - m_new)
                w_col = jnp.exp(g_col - m_new)
                vw = (v.astype(f32) * w_col).astype(bf)
                c_scr[h * ML_HP:(h + 1) * ML_HP, :] = decay * c_mem + lax.dot_general(
                    vw, k, (((0,), (0,)), ((), ())), preferred_element_type=f32)
                n_scr[h:h + 1, :] = decay * n_row + jnp.sum(k.astype(f32) * w_col, axis=0, keepdims=True)
                m_scr[h:h + 1, :] = jnp.broadcast_to(m_new, (1, LANES))
            return 0

        lax.fori_loop(0, nc_tot, chunk, 0)

    def head_out(hs, o):
        outs = []
        for h in range(ML_HEADS):
            x = hs[:, h * ML_HP:(h + 1) * ML_HP]
            ms = jnp.sum(x * x, axis=1, keepdims=True) * (1.0 / ML_HEAD_DIM)
            outs.append(x * lax.rsqrt(ms + NORM_EPS))
        hn = jnp.concatenate(outs, axis=1) * ng_ref[...]
        return (hn * jax.nn.sigmoid(o.astype(f32))).astype(bf)

    outc_ref[0] = head_out(h_scr[0:ctx_len, :], oc_ref[0])

    def fin(c, _):
        r = pl.multiple_of(c * tc, tc)
        outl_ref[0, pl.ds(r, tc), :] = head_out(h_scr[pl.ds(ctx_len + r, tc), :], ol_ref[0, pl.ds(r, tc), :])
        return 0

    lax.fori_loop(0, nc_lat, fin, 0)


def mlstm_call(qk_c, qk_l, v_c, v_l, g_c, g_l, o_c, o_l, conv_w, conv_b, gate_b, norm_g, *, row_len):
    bsz, ctx_len, _ = qk_c.shape
    seq_len = qk_l.shape[1]
    tot = ctx_len + seq_len
    per_b = lambda a: pl.BlockSpec((1,) + a.shape[1:], lambda b: (b, 0, 0))
    full = lambda a: pl.BlockSpec(a.shape, lambda b: (0, 0))
    return pl.pallas_call(
        functools.partial(_mlstm_kernel, ctx_len=ctx_len, seq_len=seq_len, row_len=row_len),
        grid=(bsz,),
        in_specs=[per_b(a) for a in (qk_c, qk_l, v_c, v_l, g_c, g_l, o_c, o_l)]
        + [full(a) for a in (conv_w, conv_b, gate_b, norm_g)],
        out_specs=[pl.BlockSpec((1, ctx_len, ML_DP), lambda b: (b, 0, 0)),
                   pl.BlockSpec((1, seq_len, ML_DP), lambda b: (b, 0, 0))],
        out_shape=[jax.ShapeDtypeStruct((bsz, ctx_len, ML_DP), jnp.bfloat16),
                   jax.ShapeDtypeStruct((bsz, seq_len, ML_DP), jnp.bfloat16)],
        scratch_shapes=[pltpu.VMEM((tot, ML_DP), jnp.bfloat16)] * 3
        + [pltpu.VMEM((tot, GATE_PAD), jnp.float32), pltpu.VMEM((tot, ML_DP), jnp.float32),
           pltpu.VMEM((ML_DP, ML_HP), jnp.float32), pltpu.VMEM((8, LANES), jnp.float32),
           pltpu.VMEM((8, LANES), jnp.float32)],
        compiler_params=pltpu.CompilerParams(
            dimension_semantics=("parallel",),
            vmem_limit_bytes=VMEM_LIMIT_BYTES),
        name="mlstm",
    )(qk_c, qk_l, v_c, v_l, g_c, g_l, o_c, o_l, conv_w, conv_b, gate_b, norm_g)


def _hyena_kernel(cw_ref, hb_ref, v_ref, x1_ref, x2_ref, fr_ref, o_ref, g_scr, *, ch, na, nb, row_len):
    bf, f32 = jnp.bfloat16, jnp.float32
    nl = na * nb
    cblk = pl.program_id(0)
    row = lax.broadcasted_iota(jnp.int32, (ch, 1), 0) % row_len
    first, last = row == 0, row == row_len - 1
    lane = lax.broadcasted_iota(jnp.int32, (1, nl), 1)
    below = lax.broadcasted_iota(jnp.int32, (ch, ch), 1) < lax.broadcasted_iota(jnp.int32, (ch, ch), 0)

    def short_conv(x_ref, j, part):
        x = x_ref[j].astype(f32)
        cidx = part * D_HY + cblk * HY_CB + j
        prev = jnp.where(first, 0.0, pltpu.roll(x, 1, 0))
        nxt = jnp.where(last, 0.0, pltpu.roll(x, ch - 1, 0))
        return cw_ref[3, cidx] + cw_ref[0, cidx] * prev + cw_ref[1, cidx] * x + cw_ref[2, cidx] * nxt

    def channel(j, _):
        z = short_conv(v_ref, j, 0)
        for o, xg_ref in ((0, x1_ref), (1, x2_ref)):
            gate = short_conv(xg_ref, j, 1 + o)
            taps = fr_ref[o, pl.ds(j, 1), :]
            r_lo = pltpu.roll(jnp.broadcast_to(taps[:, 0:ch], (ch, ch)), 0, 1, stride=1, stride_axis=0)
            for q in range(2 * na - 1):
                r_hi = pltpu.roll(jnp.broadcast_to(taps[:, (q + 1) * ch:(q + 2) * ch], (ch, ch)),
                                  0, 1, stride=1, stride_axis=0)
                g_scr[q * ch:(q + 1) * ch, :] = jnp.where(below, r_lo, r_hi).astype(bf)
                r_lo = r_hi
            zp = jnp.dot(g_scr[...], z.astype(bf), preferred_element_type=f32)
            y = zp[(na - 1) * ch:na * ch]
            for p in range(1, na):
                zpos = zp[(na - 1 - p) * ch:(na - p) * ch]
                zneg = zp[(2 * na - 1 - p) * ch:(2 * na - p) * ch]
                y = y + pltpu.roll(jnp.where(lane < (na - p) * nb, zpos, zneg), p * nb, 1)
            z = gate * (y + hb_ref[o, cblk * HY_CB + j] * z)
        o_ref[j] = z.astype(bf)
        return 0

    lax.fori_loop(0, HY_CB, channel, 0, unroll=2)


def hyena_call(conv_wb, hy_bias, ht, fr, *, ch, na, nb, row_len):
    nl = na * nb
    n_blk = D_HY // HY_CB
    part = lambda k: pl.BlockSpec((HY_CB, ch, nl), lambda i, *_: (i + k * n_blk, 0, 0))
    grid_spec = pltpu.PrefetchScalarGridSpec(
        num_scalar_prefetch=2,
        grid=(n_blk,),
        in_specs=[part(0), part(1), part(2),
                  pl.BlockSpec((HY_ORDER, HY_CB, 2 * ch * na), lambda i, *_: (0, i, 0))],
        out_specs=pl.BlockSpec((HY_CB, ch, nl), lambda i, *_: (i, 0, 0)),
        scratch_shapes=[pltpu.VMEM(((2 * na - 1) * ch, ch), jnp.bfloat16)],
    )
    return pl.pallas_call(
        functools.partial(_hyena_kernel, ch=ch, na=na, nb=nb, row_len=row_len),
        grid_spec=grid_spec,
        out_shape=jax.ShapeDtypeStruct((D_HY, ch, nl), jnp.bfloat16),
        compiler_params=pltpu.CompilerParams(
            dimension_semantics=("parallel",),
            vmem_limit_bytes=VMEM_LIMIT_BYTES),
        name="hyena",
    )(conv_wb, hy_bias, ht, ht, ht, fr)


def hyena_filters(seq_len, w1, b1, w2, b2, w3, b3, sin_freq):
    t = jnp.linspace(0.0, 1.0, seq_len, dtype=jnp.float32)[:, None]
    w = 2.0 * math.pi * jnp.arange(seq_len, dtype=jnp.float32)[:, None] / seq_len
    f = jnp.linspace(1e-4, HY_BANDS - 1, HY_BANDS, dtype=jnp.float32)[None, :]
    z = jnp.concatenate([t, jnp.cos(f * w), -jnp.sin(f * w)], axis=-1)
    hdn = jnp.sin(sin_freq[0] * (z @ w1 + b1))
    hdn = jnp.sin(sin_freq[1] * (hdn @ w2 + b2))
    hf = (hdn @ w3 + b3).astype(jnp.float32).reshape(seq_len, HY_ORDER, 2, D_HY)
    deltas = jnp.abs(jnp.linspace(HY_MIN_DECAY, HY_MAX_DECAY, D_HY, dtype=jnp.float32))
    decay = jnp.exp(-t * deltas)
    return hf * decay[:, None, None, :]


def hyena_reversed_taps(seq_len, filter_params):
    filt = hyena_filters(seq_len, *filter_params)
    out = []
    for o in range(HY_ORDER):
        h_fwd, h_bwd = filt[:, o, 0], filt[:, o, 1]
        k = jnp.concatenate([h_fwd, jnp.zeros_like(h_fwd[:1]), h_bwd[:0:-1]], axis=0)
        k = k / jnp.sum(jnp.abs(k), axis=0, keepdims=True)
        idx = (seq_len - jnp.arange(2 * seq_len)) % (2 * seq_len)
        out.append(k[idx].T)
    return jnp.stack(out)


def hyena_mixer(hy, conv_wb, hy_bias, fr, *, ch, row_len, pad_batch):
    bsz, seq_len, width = hy.shape
    na = seq_len // ch
    nb = pad_batch
    ht = hy.reshape(bsz, na, ch, width).transpose(3, 2, 1, 0)
    if nb != bsz:
        ht = jnp.pad(ht, ((0, 0), (0, 0), (0, 0), (0, nb - bsz)))
    out = hyena_call(conv_wb, hy_bias, ht.reshape(width, ch, na * nb), fr, ch=ch, na=na, nb=nb, row_len=row_len)
    out = out.reshape(D_HY, ch, na, nb)[..., :bsz]
    return out.transpose(3, 2, 1, 0).reshape(bsz, seq_len, D_HY)


def moe(h2, logits, w1p, b1g, b1l, w2, b2):
    n_tok, d = h2.shape
    i32 = jnp.int32
    top_val, top_idx = lax.top_k(logits, TOP_K)
    weights = jax.nn.softmax(top_val, axis=-1)
    flat_e = top_idx.reshape(-1).astype(i32)
    n_assign = flat_e.shape[0]
    n_blocks = -(-n_assign // MOE_ROWS) + N_EXPERTS
    iota = jnp.arange(n_assign, dtype=i32)
    experts = jnp.arange(N_EXPERTS, dtype=i32)
    e_sorted, order = lax.sort((flat_e, iota), num_keys=1, is_stable=True)
    counts = jnp.sum(flat_e[:, None] == experts[None, :], axis=0).astype(i32)
    padded = (counts + MOE_ROWS - 1) // MOE_ROWS * MOE_ROWS
    end_pad = jnp.cumsum(padded)
    start_pad = end_pad - padded
    start = jnp.cumsum(counts) - counts
    block_start = jnp.arange(n_blocks, dtype=i32) * MOE_ROWS
    block_e = jnp.minimum(jnp.sum(block_start[:, None] >= end_pad[None, :], axis=1), N_EXPERTS - 1).astype(i32)
    off = (block_start - start_pad[block_e])[:, None] + jnp.arange(MOE_ROWS, dtype=i32)[None, :]
    src = jnp.where(off < counts[block_e][:, None], start[block_e][:, None] + off, 0).reshape(-1)
    slot_tok = order[src] // TOP_K
    shift = jnp.sum(jnp.where(e_sorted[:, None] == experts[None, :], (start_pad - start)[None, :], 0), axis=1)
    _, pos = lax.sort((order, iota + shift), num_keys=1)
    n_used = (end_pad[-1] // MOE_ROWS).astype(i32).reshape(1)
    out = moe_expert_blocks(block_e, n_used, h2[slot_tok], w1p, b1g, b1l, w2, b2)
    picked = out[pos].reshape(n_tok, TOP_K, d).astype(jnp.float32)
    return jnp.sum(picked * weights[:, :, None], axis=1)


def rmsnorm(x, g):
    y = x * lax.rsqrt(jnp.mean(x * x, axis=-1, keepdims=True) + NORM_EPS)
    return y * g


def kernel(x, c, ctx, c_ctx, ada_w, ada_b, norm1_g, norm2_g, final_norm_g, w_in,
           hy_conv_w, hy_conv_b, hy_f_w1, hy_f_b1, hy_f_w2, hy_f_b2, hy_f_w3, hy_f_b3, hy_sin_freq, hy_bias,
           s5_lam_re, s5_lam_im, s5_log_dt, s5_b_re, s5_b_im, s5_c_re, s5_c_im, s5_d, s5_glu_w,
           ml_conv_w, ml_conv_b, ml_gate_b, ml_norm_g,
           w_br_hy, w_br_s5, w_br_ml, w_out,
           moe_router_w, moe_router_b, moe_w1, moe_b1, moe_w2, moe_b2):
    bsz, seq_len, d = x.shape
    ctx_len = ctx.shape[1]
    bf, f32 = jnp.bfloat16, jnp.float32
    silu_c = jax.nn.silu(c.astype(f32))
    silu_cc = jax.nn.silu(c_ctx.astype(f32))[None]
    x2 = x.reshape(bsz * seq_len, d)
    c2 = ctx.reshape(bsz * ctx_len, d)
    for l in range(DEPTH):
        need_ctx = l < DEPTH - 1
        mod_l = (silu_c @ ada_w[l] + ada_b[l])[:, None, :]
        mod_c = (silu_cc @ ada_w[l] + ada_b[l])[:, None, :]
        sh1_l, sc1_l, g1_l, sh2_l, sc2_l, g2_l = jnp.split(mod_l, 6, axis=-1)
        sh1_c, sc1_c, g1_c, sh2_c, sc2_c, g2_c = jnp.split(mod_c, 6, axis=-1)

        w_in_p = permute_w_in(w_in[l])
        gain1 = norm1_g[l].reshape(1, d)
        s5_l, qk_l, v_l, hy_l, o_l, mg_l, gt_l = norm_mod_project(x2, sh1_l, sc1_l, gain1, w_in_p, seq_len)
        s5_c, qk_c, v_c, hy_c, o_c, mg_c, gt_c = norm_mod_project(c2, sh1_c, sc1_c, gain1, w_in_p, bsz * ctx_len)
        per_b = lambda a, n: a.reshape(bsz, n, a.shape[-1])

        hy_params = (hy_f_w1[l], hy_f_b1[l], hy_f_w2[l], hy_f_b2[l], hy_f_w3[l], hy_f_b3[l], hy_sin_freq[l])
        conv_wb = jnp.concatenate([hy_conv_w[l], hy_conv_b[l][None]], axis=0)
        hy_out_l = hyena_mixer(per_b(hy_l, seq_len), conv_wb, hy_bias[l], hyena_reversed_taps(seq_len, hy_params),
                               ch=HY_CH, row_len=GRID_W, pad_batch=bsz)
        s5_w = s5_chunk_weights(s5_lam_re[l], s5_lam_im[l], s5_log_dt[l], s5_b_re[l], s5_b_im[l],
                                s5_c_re[l], s5_c_im[l], s5_d[l])
        s5_out_l, s5_out_c = s5_scan(per_b(s5_l, seq_len), per_b(s5_c, ctx_len), s5_w)
        ml_cw = jnp.concatenate([pad_heads(ml_conv_w[l][:, :D_ML]), pad_heads(ml_conv_w[l][:, D_ML:])], axis=1)
        ml_cb = jnp.concatenate([pad_heads(ml_conv_b[l][:D_ML]), pad_heads(ml_conv_b[l][D_ML:])])[None]
        ml_gb = jnp.pad(ml_gate_b[l], (0, GATE_PAD - 4 * ML_HEADS))[None]
        ml_out_c, ml_out_l = mlstm_call(
            per_b(qk_c, ctx_len), per_b(qk_l, seq_len), per_b(v_c, ctx_len), per_b(v_l, seq_len),
            per_b(gt_c, ctx_len), per_b(gt_l, seq_len), per_b(o_c, ctx_len), per_b(o_l, seq_len),
            ml_cw, ml_cb, ml_gb, pad_heads(ml_norm_g[l])[None], row_len=GRID_W)

        wglu, wh, ws, wo = s5_glu_w[l].astype(bf), w_br_hy[l].astype(bf), w_br_s5[l].astype(bf), w_out[l].astype(bf)
        wm = pad_heads(w_br_ml[l].T).T.astype(bf)
        rw = jnp.pad(moe_router_w[l], ((0, 0), (0, ROUTER_PAD - N_EXPERTS))).astype(bf)
        rb = jnp.pad(moe_router_b[l], (0, ROUTER_PAD - N_EXPERTS))[None]
        n2 = norm2_g[l].reshape(1, d)
        flat = lambda a: a.reshape(-1, a.shape[-1])
        x2, h2_l, lg_l = merge_project_residual(
            flat(hy_out_l), flat(s5_out_l), flat(ml_out_l), mg_l, x2, g1_l, sh2_l, sc2_l, n2,
            wglu, wh, ws, wm, wo, rw, rb, seq_len)

        w1p = deinterleave_cast(moe_w1[l].reshape(N_EXPERTS * d, 2 * D_FF_EXPERT))
        moe_w = (w1p.reshape(N_EXPERTS, d, 2 * D_FF_EXPERT),
                 moe_b1[l][:, None, 0::2], moe_b1[l][:, None, 1::2],
                 moe_w2[l].astype(bf), moe_b2[l][:, None, :])
        if need_ctx:
            hy_out_c = hyena_mixer(per_b(hy_c, ctx_len), conv_wb, hy_bias[l],
                                   hyena_reversed_taps(ctx_len, hy_params),
                                   ch=ctx_len, row_len=ctx_len, pad_batch=LANES)
            c2, h2_c, lg_c = merge_project_residual(
                flat(hy_out_c), flat(s5_out_c), flat(ml_out_c), mg_c, c2, g1_c, sh2_c, sc2_c, n2,
                wglu, wh, ws, wm, wo, rw, rb, bsz * ctx_len)
            tok = jnp.concatenate([h2_c.reshape(bsz, ctx_len, d), h2_l.reshape(bsz, seq_len, d)], axis=1)
            lg = jnp.concatenate([lg_c.reshape(bsz, ctx_len, -1), lg_l.reshape(bsz, seq_len, -1)], axis=1)
            f = moe(tok.reshape(-1, d), lg.reshape(-1, ROUTER_PAD)[:, :N_EXPERTS], *moe_w)
            f = f.reshape(bsz, ctx_len + seq_len, d)
            c2 = c2 + (g2_c * f[:, :ctx_len]).reshape(-1, d)
            x2 = x2 + (g2_l * f[:, ctx_len:]).reshape(-1, d)
        else:
            f = moe(h2_l, lg_l[:, :N_EXPERTS], *moe_w)
            x2 = x2 + (g2_l * f.reshape(bsz, seq_len, d)).reshape(-1, d)
    return rmsnorm(x2, final_norm_g).reshape(bsz, seq_len, d)
```

```python
import functools
import math

import jax
import jax.numpy as jnp
from jax import lax
from jax.experimental import pallas as pl
from jax.experimental.pallas import tpu as pltpu

D_MODEL = 1024
DEPTH = 4
GRID_W = 64

D_HY = 384
D_S5 = 384
D_ML = 384
N_BRANCH = 3
SHORT_CONV = 3

HY_ORDER = 2
HY_EMB = 33
HY_BANDS = (HY_EMB - 1) // 2
HY_DECAY_TARGET = 1e-2
HY_FAST_DECAY_PCT = 0.3
HY_SLOW_DECAY_PCT = 1.5
HY_MIN_DECAY = math.log(HY_DECAY_TARGET) / HY_SLOW_DECAY_PCT
HY_MAX_DECAY = math.log(HY_DECAY_TARGET) / HY_FAST_DECAY_PCT

S5_GROUP = 16
S5_GROUPS = D_S5 // S5_GROUP
S5_STATE = 64

ML_HEADS = 4
ML_HEAD_DIM = D_ML // ML_HEADS
NEG = -1e30

N_EXPERTS = 32
TOP_K = 4
D_FF_EXPERT = 512
SWIGLU_LIMIT = 7.0
SWIGLU_ALPHA = 1.702

NORM_EPS = 1e-6

COL_SIZES = (D_S5, 2 * D_ML, D_ML, 4 * ML_HEADS, (1 + HY_ORDER) * D_HY, D_ML, N_BRANCH * D_MODEL)

LANES = 128
VMEM_LIMIT_BYTES = 56 * 1024 * 1024

ML_HP = LANES
ML_DP = ML_HEADS * ML_HP
ML_TC = 128
GATE_PAD = LANES

PROJ_OUT = (("s5", D_S5, jnp.bfloat16), ("qk", 2 * ML_DP, jnp.bfloat16), ("v", ML_DP, jnp.bfloat16),
            ("hy", (1 + HY_ORDER) * D_HY, jnp.bfloat16), ("o", ML_DP, jnp.bfloat16),
            ("mg", N_BRANCH * D_MODEL, jnp.bfloat16), ("gt", GATE_PAD, jnp.float32))
PROJ_COLS = sum(w for _, w, _ in PROJ_OUT)
PROJ_TM = 512

MOE_ROWS = 512
MOE_SPLIT = 2
MERGE_TM = 512
ROUTER_PAD = LANES

HY_CB = 8
HY_CH = 128


def _proj_kernel(x_ref, shift_ref, scale_ref, g_ref, w_ref, *o_refs):
    x = x_ref[...]
    ms = jnp.mean(x * x, axis=-1, keepdims=True)
    y = x * lax.rsqrt(ms + NORM_EPS) * g_ref[...]
    h = (y * (1.0 + scale_ref[0]) + shift_ref[0]).astype(jnp.bfloat16)
    start = 0
    for o_ref, (_, width, _) in zip(o_refs, PROJ_OUT):
        o_ref[...] = jnp.dot(h, w_ref[:, start:start + width],
                             preferred_element_type=jnp.float32).astype(o_ref.dtype)
        start += width


def norm_mod_project(x2d, shift, scale, gain, w_bf16, rows_per_mod):
    rows, d = x2d.shape
    tm = PROJ_TM
    blocks_per_mod = rows_per_mod // tm
    return pl.pallas_call(
        _proj_kernel,
        grid=(rows // tm,),
        in_specs=[
            pl.BlockSpec((tm, d), lambda i: (i, 0)),
            pl.BlockSpec((1, 1, d), lambda i: (i // blocks_per_mod, 0, 0)),
            pl.BlockSpec((1, 1, d), lambda i: (i // blocks_per_mod, 0, 0)),
            pl.BlockSpec((1, d), lambda i: (0, 0)),
            pl.BlockSpec((d, PROJ_COLS), lambda i: (0, 0)),
        ],
        out_specs=[pl.BlockSpec((tm, w), lambda i: (i, 0)) for _, w, _ in PROJ_OUT],
        out_shape=[jax.ShapeDtypeStruct((rows, w), dt) for _, w, dt in PROJ_OUT],
        compiler_params=pltpu.CompilerParams(
            dimension_semantics=("parallel",),
            vmem_limit_bytes=VMEM_LIMIT_BYTES),
        name="norm_mod_project",
    )(x2d, shift, scale, gain, w_bf16)


def pad_heads(w):
    lead = w.shape[:-1]
    w = w.reshape(lead + (ML_HEADS, ML_HEAD_DIM))
    w = jnp.pad(w, [(0, 0)] * len(lead) + [(0, 0), (0, ML_HP - ML_HEAD_DIM)])
    return w.reshape(lead + (ML_DP,))


def permute_w_in(w_in):
    cols, start = [], 0
    for s in COL_SIZES:
        cols.append(w_in[:, start:start + s])
        start += s
    s5, qk, v, gt, hy, o, mg = cols
    qk = jnp.concatenate([pad_heads(qk[:, :D_ML]), pad_heads(qk[:, D_ML:])], axis=1)
    gt = jnp.pad(gt, ((0, 0), (0, GATE_PAD - gt.shape[1])))
    return jnp.concatenate([s5, qk, pad_heads(v), hy, pad_heads(o), mg, gt], axis=1).astype(jnp.bfloat16)


def _merge_kernel(hy_ref, s5_ref, ml_ref, mg_ref, x_ref, g1_ref, sh2_ref, sc2_ref, n2_ref,
                  wglu_ref, wh_ref, ws_ref, wm_ref, wo_ref, rw_ref, rb_ref,
                  xo_ref, h2_ref, lg_ref):
    bf = jnp.bfloat16
    f32 = jnp.float32
    d = D_MODEL
    g = jax.nn.gelu(s5_ref[...].astype(f32))
    s5 = g * jax.nn.sigmoid(jnp.dot(g.astype(bf), wglu_ref[...], preferred_element_type=f32))
    y = jax.nn.sigmoid(mg_ref[:, 0:d].astype(f32)) * jnp.dot(hy_ref[...], wh_ref[...], preferred_element_type=f32)
    y = y + jax.nn.sigmoid(mg_ref[:, d:2 * d].astype(f32)) * jnp.dot(s5.astype(bf), ws_ref[...],
                                                                     preferred_element_type=f32)
    y = y + jax.nn.sigmoid(mg_ref[:, 2 * d:3 * d].astype(f32)) * jnp.dot(ml_ref[...], wm_ref[...],
                                                                         preferred_element_type=f32)
    out = jnp.dot(y.astype(bf), wo_ref[...], preferred_element_type=f32)
    xn = x_ref[...] + g1_ref[0] * out
    xo_ref[...] = xn
    ms = jnp.mean(xn * xn, axis=-1, keepdims=True)
    h2 = (xn * lax.rsqrt(ms + NORM_EPS) * n2_ref[...]) * (1.0 + sc2_ref[0]) + sh2_ref[0]
    h2b = h2.astype(bf)
    h2_ref[...] = h2b
    lg_ref[...] = jnp.dot(h2b, rw_ref[...], preferred_element_type=f32) + rb_ref[...]


def merge_project_residual(hy, s5, ml, mg, x2d, g1, sh2, sc2, n2, wglu, wh, ws, wm, wo, rw, rb, rows_per_mod):
    rows, d = x2d.shape
    tm = MERGE_TM
    blocks_per_mod = rows_per_mod // tm
    row_spec = lambda c: pl.BlockSpec((tm, c), lambda i: (i, 0))
    mod_spec = pl.BlockSpec((1, 1, d), lambda i: (i // blocks_per_mod, 0, 0))
    full = lambda a: pl.BlockSpec(a.shape, lambda i: (0, 0))
    return pl.pallas_call(
        _merge_kernel,
        grid=(rows // tm,),
        in_specs=[row_spec(D_HY), row_spec(D_S5), row_spec(ML_DP), row_spec(N_BRANCH * d), row_spec(d),
                  mod_spec, mod_spec, mod_spec, full(n2),
                  full(wglu), full(wh), full(ws), full(wm), full(wo), full(rw), full(rb)],
        out_specs=[row_spec(d), row_spec(d), row_spec(ROUTER_PAD)],
        out_shape=[jax.ShapeDtypeStruct((rows, d), jnp.float32),
                   jax.ShapeDtypeStruct((rows, d), jnp.bfloat16),
                   jax.ShapeDtypeStruct((rows, ROUTER_PAD), jnp.float32)],
        compiler_params=pltpu.CompilerParams(
            dimension_semantics=("parallel",),
            vmem_limit_bytes=VMEM_LIMIT_BYTES),
        name="merge_project_residual",
    )(hy, s5, ml, mg, x2d, g1, sh2, sc2, n2, wglu, wh, ws, wm, wo, rw, rb)


def _deinterleave_kernel(w_ref, p_ref, o_ref):
    o_ref[...] = jnp.dot(w_ref[...].astype(jnp.bfloat16), p_ref[...],
                         preferred_element_type=jnp.float32).astype(jnp.bfloat16)


def deinterleave_cast(w_all, layer):
    rows, n = w_all.shape[0] // DEPTH, w_all.shape[1]
    tm = 1024
    first = layer * (rows // tm)
    src = jnp.concatenate([jnp.arange(0, n, 2), jnp.arange(1, n, 2)])
    perm = (jnp.arange(n)[:, None] == src[None, :]).astype(jnp.bfloat16)
    return pl.pallas_call(
        _deinterleave_kernel,
        grid=(rows // tm,),
        in_specs=[pl.BlockSpec((tm, n), lambda i: (first + i, 0)), pl.BlockSpec((n, n), lambda i: (0, 0))],
        out_specs=pl.BlockSpec((tm, n), lambda i: (i, 0)),
        out_shape=jax.ShapeDtypeStruct((rows, n), jnp.bfloat16),
        compiler_params=pltpu.CompilerParams(
            dimension_semantics=("parallel",),
            vmem_limit_bytes=VMEM_LIMIT_BYTES),
        name="deinterleave_cast",
    )(w_all, perm)


def _moe_kernel(be_ref, nu_ref, x_ref, w1_ref, b1g_ref, b1l_ref, w2_ref, b2_ref, o_ref):
    f = D_FF_EXPERT

    @pl.when(pl.program_id(0) < nu_ref[0])
    def _():
        x = x_ref[...]
        hg = jnp.dot(x, w1_ref[0, :, 0:f], preferred_element_type=jnp.float32) + b1g_ref[0]
        hl = jnp.dot(x, w1_ref[0, :, f:2 * f], preferred_element_type=jnp.float32) + b1l_ref[0]
        x_glu = jnp.minimum(hg, SWIGLU_LIMIT)
        x_lin = jnp.clip(hl, -SWIGLU_LIMIT, SWIGLU_LIMIT)
        act = x_glu * jax.nn.sigmoid(SWIGLU_ALPHA * x_glu) * (x_lin + 1.0)
        out = jnp.dot(act.astype(jnp.bfloat16), w2_ref[0], preferred_element_type=jnp.float32) + b2_ref[0]
        o_ref[...] = out.astype(o_ref.dtype)


def moe_expert_blocks(block_e, n_used, x_sorted, w1p, b1g, b1l, w2, b2):
    n_slots, d = x_sorted.shape
    n_blocks = n_slots // MOE_ROWS
    f = D_FF_EXPERT

    def row_map(i, be, nu):
        return (jnp.maximum(jnp.minimum(i, nu[0] - 1), 0), 0)

    def w_map(i, be, nu):
        return (be[jnp.maximum(jnp.minimum(i, nu[0] - 1), 0)], 0, 0)

    grid_spec = pltpu.PrefetchScalarGridSpec(
        num_scalar_prefetch=2,
        grid=(n_blocks,),
        in_specs=[
            pl.BlockSpec((MOE_ROWS, d), row_map),
            pl.BlockSpec((1, d, 2 * f), w_map),
            pl.BlockSpec((1, 1, f), w_map),
            pl.BlockSpec((1, 1, f), w_map),
            pl.BlockSpec((1, f, d), w_map),
            pl.BlockSpec((1, 1, d), w_map),
        ],
        out_specs=pl.BlockSpec((MOE_ROWS, d), row_map),
    )
    return pl.pallas_call(
        _moe_kernel,
        grid_spec=grid_spec,
        out_shape=jax.ShapeDtypeStruct((n_slots, d), jnp.bfloat16),
        compiler_params=pltpu.CompilerParams(
            dimension_semantics=("arbitrary",),
            vmem_limit_bytes=VMEM_LIMIT_BYTES),
        name="moe_expert_blocks",
    )(block_e, n_used, x_sorted, w1p, b1g, b1l, w2, b2)


S5_T = 16
S5_W = S5_T * S5_GROUP
S5_HALF = 2 * S5_STATE


def _s5_kernel(u_ref, wp_ref, wm_ref, wq0_ref, wq1_ref, ar_ref, ai_ref, y_ref, s_scr, x_scr, yb_scr, ur_scr,
               *, n_batch, ctx_chunks, lat_chunks):
    nb = n_batch
    u = u_ref[0]
    segments = ((0, ctx_chunks), (ctx_chunks * nb, lat_chunks))
    for seg_start, seg_chunks in segments:
        for i in range(seg_chunks):
            dst = seg_start + nb * i
            src = seg_start + nb * (seg_chunks - 1 - i)
            ur_scr[dst:dst + nb, :] = u_ref[0, src:src + nb, :]
    s_scr[...] = (jnp.dot(u, wp_ref[0, 0:S5_W, :], preferred_element_type=jnp.float32)
                  + jnp.dot(ur_scr[...], wp_ref[0, S5_W:2 * S5_W, :], preferred_element_type=jnp.float32))
    ar = jnp.broadcast_to(ar_ref[0], (nb, S5_HALF))
    ai = jnp.broadcast_to(ai_ref[0], (nb, S5_HALF))

    def step(i, carry):
        xr, xi = carry
        r = pl.multiple_of(i * nb, nb)
        x_scr[pl.ds(r, nb), 0:S5_HALF] = xr
        x_scr[pl.ds(r, nb), S5_HALF:2 * S5_HALF] = xi
        sr = s_scr[pl.ds(r, nb), 0:S5_HALF]
        si = s_scr[pl.ds(r, nb), S5_HALF:2 * S5_HALF]
        return ar * xr - ai * xi + sr, ar * xi + ai * xr + si

    zero = jnp.zeros((nb, S5_HALF), jnp.float32)
    lax.fori_loop(0, ctx_chunks + lat_chunks, step, (zero, zero))

    xin = x_scr[...].astype(jnp.bfloat16)
    s_scr[...] = (jnp.dot(u, wm_ref[0], preferred_element_type=jnp.float32)
                  + jnp.dot(xin, wq0_ref[0], preferred_element_type=jnp.float32))
    yb_scr[...] = jnp.dot(xin, wq1_ref[0], preferred_element_type=jnp.float32)
    for seg_start, seg_chunks in segments:
        for i in range(seg_chunks):
            dst = seg_start + nb * i
            src = seg_start + nb * (seg_chunks - 1 - i)
            y_ref[0, dst:dst + nb, :] = (s_scr[dst:dst + nb, :] + yb_scr[src:src + nb, :]).astype(y_ref.dtype)


def s5_scan_call(u, wp, wm, wq0, wq1, ar, ai, *, n_batch, ctx_chunks):
    groups, rows, _ = u.shape
    lat_chunks = rows // n_batch - ctx_chunks
    per_group = lambda a: pl.BlockSpec((1,) + a.shape[1:], lambda g: (g, 0, 0))
    return pl.pallas_call(
        functools.partial(_s5_kernel, n_batch=n_batch, ctx_chunks=ctx_chunks, lat_chunks=lat_chunks),
        grid=(groups,),
        in_specs=[per_group(a) for a in (u, wp, wm, wq0, wq1, ar, ai)],
        out_specs=pl.BlockSpec((1, rows, S5_W), lambda g: (g, 0, 0)),
        out_shape=jax.ShapeDtypeStruct((groups, rows, S5_W), jnp.bfloat16),
        scratch_shapes=[pltpu.VMEM((rows, S5_W), jnp.float32)] * 3 + [pltpu.VMEM((rows, S5_W), jnp.bfloat16)],
        compiler_params=pltpu.CompilerParams(
            dimension_semantics=("parallel",),
            vmem_limit_bytes=VMEM_LIMIT_BYTES),
        name="s5_scan",
    )(u, wp, wm, wq0, wq1, ar, ai)


def s5_chunk_weights(lam_re, lam_im, log_dt, b_re, b_im, c_re, c_im, d_skip):
    f32 = jnp.float32
    t_len, g_n, p_n, n_n = S5_T, S5_GROUPS, S5_STATE, S5_GROUP
    b_mat = lax.complex(b_re.astype(f32), b_im.astype(f32))
    c_mat = lax.complex(c_re.astype(f32), c_im.astype(f32))
    lam = lax.complex(lam_re.astype(f32), lam_im.astype(f32))
    lam_dt = lam * jnp.exp(log_dt.astype(f32))[..., None]
    b_bar = ((jnp.exp(lam_dt) - 1.0) / lam)[..., None] * b_mat
    j = jnp.arange(t_len + 1, dtype=f32)
    pw = jnp.exp(j[:, None, None, None] * lam_dt[None])

    kern = [jnp.real(jnp.einsum('gnp,jgp,gpm->jgnm', c_mat, pw[:t_len, d], b_bar[d])) for d in range(2)]
    idx = jnp.arange(t_len)
    diff = idx[None, :] - idx[:, None]
    k0 = kern[0][jnp.clip(diff, 0, None)]
    k1 = kern[1][jnp.clip(-diff, 0, None)]
    m5 = (jnp.where((diff >= 0)[:, :, None, None, None], k0, 0.0)
          + jnp.where((diff <= 0)[:, :, None, None, None], k1, 0.0))
    skip = jnp.eye(t_len, dtype=f32)[:, :, None, None, None] * (
        d_skip.astype(f32).reshape(g_n, n_n)[None, None, :, :, None] * jnp.eye(n_n, dtype=f32)[None, None, None])
    wm = (m5 + skip).transpose(2, 0, 4, 1, 3).reshape(g_n, S5_W, S5_W)

    pf = jnp.einsum('sgp,gpm->gsmp', pw[t_len - 1 - idx, 0], b_bar[0]).reshape(g_n, S5_W, p_n)
    pb = jnp.einsum('sgp,gpm->gsmp', pw[idx, 1], b_bar[1]).reshape(g_n, S5_W, p_n)
    z = jnp.zeros_like(jnp.real(pf))
    wp = jnp.concatenate([
        jnp.concatenate([jnp.real(pf), z, jnp.imag(pf), z], axis=-1),
        jnp.concatenate([z, jnp.real(pb), z, jnp.imag(pb)], axis=-1)], axis=1)

    q0 = jnp.einsum('gnp,tgp->gptn', c_mat, pw[idx + 1, 0]).reshape(g_n, p_n, S5_W)
    q1 = jnp.einsum('gnp,tgp->gptn', c_mat, pw[t_len - idx, 1]).reshape(g_n, p_n, S5_W)
    zq = jnp.zeros_like(jnp.real(q0))
    wq0 = jnp.concatenate([jnp.real(q0), zq, -jnp.imag(q0), zq], axis=1)
    wq1 = jnp.concatenate([zq, jnp.real(q1), zq, -jnp.imag(q1)], axis=1)
    lam_t = pw[t_len]
    ar = jnp.concatenate([jnp.real(lam_t[0]), jnp.real(lam_t[1])], axis=-1)[:, None, :]
    ai = jnp.concatenate([jnp.imag(lam_t[0]), jnp.imag(lam_t[1])], axis=-1)[:, None, :]
    bf = jnp.bfloat16
    return wp.astype(bf), wm.astype(bf), wq0.astype(bf), wq1.astype(bf), ar, ai


def s5_scan(u_l, u_c, weights):
    bsz, seq_len, _ = u_l.shape
    ctx_len = u_c.shape[1]

    def to_chunks(a):
        a = a.reshape(bsz, a.shape[1] // S5_T, S5_T, S5_GROUPS, S5_GROUP).transpose(3, 1, 0, 2, 4)
        return a.reshape(S5_GROUPS, -1, S5_W)

    u = jnp.concatenate([to_chunks(u_c), to_chunks(u_l)], axis=1).astype(jnp.bfloat16)
    y = s5_scan_call(u, *weights, n_batch=bsz, ctx_chunks=ctx_len // S5_T)

    def from_chunks(a, length):
        a = a.reshape(S5_GROUPS, length // S5_T, bsz, S5_T, S5_GROUP)
        return a.transpose(2, 1, 3, 0, 4).reshape(bsz, length, D_S5)

    ctx_rows = ctx_len // S5_T * bsz
    return from_chunks(y[:, ctx_rows:], seq_len), from_chunks(y[:, :ctx_rows], ctx_len)


def _dot01(a01, x):
    bf, f32 = jnp.bfloat16, jnp.float32
    hi = x.astype(bf)
    r1 = x - hi.astype(f32)
    mid = r1.astype(bf)
    lo = (r1 - mid.astype(f32)).astype(bf)
    d = lambda y: jnp.dot(a01, y, preferred_element_type=f32)
    return d(hi) + d(mid) + d(lo)


def _mlstm_kernel(qkc_ref, qkl_ref, vc_ref, vl_ref, gc_ref, gl_ref, oc_ref, ol_ref,
                  cw_ref, cb_ref, gb_ref, ng_ref,
                  outc_ref, outl_ref,
                  q_scr, k_scr, v_scr, g_scr, h_scr, c_scr, n_scr, m_scr,
                  *, ctx_len, seq_len, row_len):
    bf, f32 = jnp.bfloat16, jnp.float32
    tc = ML_TC
    nc_ctx = ctx_len // tc
    nc_lat = seq_len // tc
    nc_tot = nc_ctx + nc_lat

    def conv_silu(x, n_rows, rlen):
        row = lax.broadcasted_iota(jnp.int32, (n_rows, 1), 0) % rlen
        prev = jnp.where(row == 0, 0.0, pltpu.roll(x, 1, 0))
        nxt = jnp.where(row == rlen - 1, 0.0, pltpu.roll(x, n_rows - 1, 0))
        y = cb_ref[...] + prev * cw_ref[0:1, :] + x * cw_ref[1:2, :] + nxt * cw_ref[2:3, :]
        return y * jax.nn.sigmoid(y)

    kscale = ML_HEAD_DIM ** -0.5
    a = conv_silu(qkc_ref[0].astype(f32), ctx_len, ctx_len)
    q_scr[0:ctx_len, :] = a[:, 0:ML_DP].astype(bf)
    k_scr[0:ctx_len, :] = (a[:, ML_DP:2 * ML_DP] * kscale).astype(bf)
    v_scr[0:ctx_len, :] = vc_ref[0]
    g_scr[0:ctx_len, :] = gc_ref[0] + gb_ref[...]

    def prep(c, _):
        r = pl.multiple_of(c * tc, tc)
        a = conv_silu(qkl_ref[0, pl.ds(r, tc), :].astype(f32), tc, row_len)
        q_scr[pl.ds(ctx_len + r, tc), :] = a[:, 0:ML_DP].astype(bf)
        k_scr[pl.ds(ctx_len + r, tc), :] = (a[:, ML_DP:2 * ML_DP] * kscale).astype(bf)
        v_scr[pl.ds(ctx_len + r, tc), :] = vl_ref[0, pl.ds(r, tc), :]
        g_scr[pl.ds(ctx_len + r, tc), :] = gl_ref[0, pl.ds(r, tc), :] + gb_ref[...]
        return 0

    lax.fori_loop(0, nc_lat, prep, 0)

    ti = lax.broadcasted_iota(jnp.int32, (tc, tc), 0)
    si = lax.broadcasted_iota(jnp.int32, (tc, tc), 1)

    for d in range(2):
        causal = (si <= ti) if d == 0 else (si >= ti)
        tri = causal.astype(bf)
        c_scr[...] = jnp.zeros_like(c_scr)
        n_scr[...] = jnp.zeros_like(n_scr)
        m_scr[...] = jnp.full_like(m_scr, NEG)

        def chunk(i, _, d=d, causal=causal, tri=tri):
            if d == 0:
                ci = i
            else:
                ci = jnp.where(i < nc_ctx, nc_ctx - 1 - i, nc_tot + nc_ctx - 1 - i)
            r0 = pl.multiple_of(ci * tc, tc)
            gts = g_scr[pl.ds(r0, tc), :]
            logf = jax.nn.log_sigmoid(gts)
            bm = _dot01(tri, logf)
            gts_t = gts.T
            bm_t = bm.T
            b_end_row = bm[tc - 1:tc, :] if d == 0 else bm[0:1, :]
            for h in range(ML_HEADS):
                icol = h + 8 * d
                fcol = ML_HEADS + h + 8 * d
                lanes = slice(h * ML_HP, (h + 1) * ML_HP)
                i_col = gts[:, icol:icol + 1]
                b_col = bm[:, fcol:fcol + 1]
                i_row = gts_t[icol:icol + 1, :]
                b_row = bm_t[fcol:fcol + 1, :]
                b_end = b_end_row[:, fcol:fcol + 1]
                m_prev = m_scr[h:h + 1, 0:1]
                dmat = jnp.where(causal, b_col - b_row + i_row, NEG)
                a_col = b_col + m_prev
                m_out = jnp.maximum(a_col, jnp.max(dmat, axis=1, keepdims=True))
                wmat = jnp.exp(dmat - m_out)
                wa = jnp.exp(a_col - m_out)
                q = q_scr[pl.ds(r0, tc), lanes]
                k = k_scr[pl.ds(r0, tc), lanes]
                v = v_scr[pl.ds(r0, tc), lanes]
                s = lax.dot_general(q, k, (((1,), (1,)), ((), ())), preferred_element_type=f32) * wmat
                c_mem = c_scr[h * ML_HP:(h + 1) * ML_HP, :]
                n_row = n_scr[h:h + 1, :]
                qf = q.astype(f32)
                num = (jnp.dot(s.astype(bf), v, preferred_element_type=f32)
                       + wa * lax.dot_general(q, c_mem.astype(bf), (((1,), (1,)), ((), ())),
                                              preferred_element_type=f32))
                den = jnp.sum(s, axis=1, keepdims=True) + wa * jnp.sum(qf * n_row, axis=1, keepdims=True)
                hval = num / jnp.maximum(jnp.abs(den), jnp.exp(-m_out))
                if d == 0:
                    h_scr[pl.ds(r0, tc), lanes] = hval
                else:
                    h_scr[pl.ds(r0, tc), lanes] += hval
                g_col = b_end - b_col + i_col
                m_new = jnp.maximum(b_end + m_prev, jnp.max(g_col, axis=0, keepdims=True))
                decay = jnp.exp(b_end + m_prev - m_new)
                w_col = jnp.exp(g_col - m_new)
                vw = (v.astype(f32) * w_col).astype(bf)
                c_scr[h * ML_HP:(h + 1) * ML_HP, :] = decay * c_mem + lax.dot_general(
                    vw, k, (((0,), (0,)), ((), ())), preferred_element_type=f32)
                n_scr[h:h + 1, :] = decay * n_row + jnp.sum(k.astype(f32) * w_col, axis=0, keepdims=True)
                m_scr[h:h + 1, :] = jnp.broadcast_to(m_new, (1, LANES))
            return 0

        lax.fori_loop(0, nc_tot, chunk, 0)

    def head_out(hs, o):
        outs = []
        for h in range(ML_HEADS):
            x = hs[:, h * ML_HP:(h + 1) * ML_HP]
            ms = jnp.sum(x * x, axis=1, keepdims=True) * (1.0 / ML_HEAD_DIM)
            outs.append(x * lax.rsqrt(ms + NORM_EPS))
        hn = jnp.concatenate(outs, axis=1) * ng_ref[...]
        return (hn * jax.nn.sigmoid(o.astype(f32))).astype(bf)

    outc_ref[0] = head_out(h_scr[0:ctx_len, :], oc_ref[0])

    def fin(c, _):
        r = pl.multiple_of(c * tc, tc)
        outl_ref[0, pl.ds(r, tc), :] = head_out(h_scr[pl.ds(ctx_len + r, tc), :], ol_ref[0, pl.ds(r, tc), :])
        return 0

    lax.fori_loop(0, nc_lat, fin, 0)


def mlstm_call(qk_c, qk_l, v_c, v_l, g_c, g_l, o_c, o_l, conv_w, conv_b, gate_b, norm_g, *, row_len):
    bsz, ctx_len, _ = qk_c.shape
    seq_len = qk_l.shape[1]
    tot = ctx_len + seq_len
    per_b = lambda a: pl.BlockSpec((1,) + a.shape[1:], lambda b: (b, 0, 0))
    full = lambda a: pl.BlockSpec(a.shape, lambda b: (0, 0))
    return pl.pallas_call(
        functools.partial(_mlstm_kernel, ctx_len=ctx_len, seq_len=seq_len, row_len=row_len),
        grid=(bsz,),
        in_specs=[per_b(a) for a in (qk_c, qk_l, v_c, v_l, g_c, g_l, o_c, o_l)]
        + [full(a) for a in (conv_w, conv_b, gate_b, norm_g)],
        out_specs=[pl.BlockSpec((1, ctx_len, ML_DP), lambda b: (b, 0, 0)),
                   pl.BlockSpec((1, seq_len, ML_DP), lambda b: (b, 0, 0))],
        out_shape=[jax.ShapeDtypeStruct((bsz, ctx_len, ML_DP), jnp.bfloat16),
                   jax.ShapeDtypeStruct((bsz, seq_len, ML_DP), jnp.bfloat16)],
        scratch_shapes=[pltpu.VMEM((tot, ML_DP), jnp.bfloat16)] * 3
        + [pltpu.VMEM((tot, GATE_PAD), jnp.float32), pltpu.VMEM((tot, ML_DP), jnp.float32),
           pltpu.VMEM((ML_DP, ML_HP), jnp.float32), pltpu.VMEM((8, LANES), jnp.float32),
           pltpu.VMEM((8, LANES), jnp.float32)],
        compiler_params=pltpu.CompilerParams(
            dimension_semantics=("parallel",),
            vmem_limit_bytes=VMEM_LIMIT_BYTES),
        name="mlstm",
    )(qk_c, qk_l, v_c, v_l, g_c, g_l, o_c, o_l, conv_w, conv_b, gate_b, norm_g)


def _hyena_kernel(cw_ref, hb_ref, v_ref, x1_ref, x2_ref, fr_ref, o_ref, g_scr, *, ch, na, nb, row_len):
    bf, f32 = jnp.bfloat16, jnp.float32
    nl = na * nb
    cblk = pl.program_id(0)
    row = lax.broadcasted_iota(jnp.int32, (ch, 1), 0) % row_len
    first, last = row == 0, row == row_len - 1
    lane = lax.broadcasted_iota(jnp.int32, (1, nl), 1)
    below = lax.broadcasted_iota(jnp.int32, (ch, ch), 1) < lax.broadcasted_iota(jnp.int32, (ch, ch), 0)

    def short_conv(x_ref, j, part):
        x = x_ref[j].astype(f32)
        cidx = part * D_HY + cblk * HY_CB + j
        prev = jnp.where(first, 0.0, pltpu.roll(x, 1, 0))
        nxt = jnp.where(last, 0.0, pltpu.roll(x, ch - 1, 0))
        return cw_ref[3, cidx] + cw_ref[0, cidx] * prev + cw_ref[1, cidx] * x + cw_ref[2, cidx] * nxt

    def channel(j, _):
        z = short_conv(v_ref, j, 0)
        for o, xg_ref in ((0, x1_ref), (1, x2_ref)):
            gate = short_conv(xg_ref, j, 1 + o)
            taps = fr_ref[o, pl.ds(j, 1), :]
            r_lo = pltpu.roll(jnp.broadcast_to(taps[:, 0:ch], (ch, ch)), 0, 1, stride=1, stride_axis=0)
            for q in range(2 * na - 1):
                r_hi = pltpu.roll(jnp.broadcast_to(taps[:, (q + 1) * ch:(q + 2) * ch], (ch, ch)),
                                  0, 1, stride=1, stride_axis=0)
                g_scr[q * ch:(q + 1) * ch, :] = jnp.where(below, r_lo, r_hi).astype(bf)
                r_lo = r_hi
            zp = jnp.dot(g_scr[...], z.astype(bf), preferred_element_type=f32)
            y = zp[(na - 1) * ch:na * ch]
            for p in range(1, na):
                zpos = zp[(na - 1 - p) * ch:(na - p) * ch]
                zneg = zp[(2 * na - 1 - p) * ch:(2 * na - p) * ch]
                y = y + pltpu.roll(jnp.where(lane < (na - p) * nb, zpos, zneg), p * nb, 1)
            z = gate * (y + hb_ref[o, cblk * HY_CB + j] * z)
        o_ref[j] = z.astype(bf)
        return 0

    lax.fori_loop(0, HY_CB, channel, 0, unroll=2)


def hyena_call(conv_wb, hy_bias, ht, fr, *, ch, na, nb, row_len):
    nl = na * nb
    n_blk = D_HY // HY_CB
    part = lambda k: pl.BlockSpec((HY_CB, ch, nl), lambda i, *_: (i + k * n_blk, 0, 0))
    grid_spec = pltpu.PrefetchScalarGridSpec(
        num_scalar_prefetch=2,
        grid=(n_blk,),
        in_specs=[part(0), part(1), part(2),
                  pl.BlockSpec((HY_ORDER, HY_CB, 2 * ch * na), lambda i, *_: (0, i, 0))],
        out_specs=pl.BlockSpec((HY_CB, ch, nl), lambda i, *_: (i, 0, 0)),
        scratch_shapes=[pltpu.VMEM(((2 * na - 1) * ch, ch), jnp.bfloat16)],
    )
    return pl.pallas_call(
        functools.partial(_hyena_kernel, ch=ch, na=na, nb=nb, row_len=row_len),
        grid_spec=grid_spec,
        out_shape=jax.ShapeDtypeStruct((D_HY, ch, nl), jnp.bfloat16),
        compiler_params=pltpu.CompilerParams(
            dimension_semantics=("parallel",),
            vmem_limit_bytes=VMEM_LIMIT_BYTES),
        name="hyena",
    )(conv_wb, hy_bias, ht, ht, ht, fr)


def hyena_filters(seq_len, w1, b1, w2, b2, w3, b3, sin_freq):
    t = jnp.linspace(0.0, 1.0, seq_len, dtype=jnp.float32)[:, None]
    w = 2.0 * math.pi * jnp.arange(seq_len, dtype=jnp.float32)[:, None] / seq_len
    f = jnp.linspace(1e-4, HY_BANDS - 1, HY_BANDS, dtype=jnp.float32)[None, :]
    z = jnp.concatenate([t, jnp.cos(f * w), -jnp.sin(f * w)], axis=-1)
    hdn = jnp.sin(sin_freq[0] * (z @ w1 + b1))
    hdn = jnp.sin(sin_freq[1] * (hdn @ w2 + b2))
    hf = (hdn @ w3 + b3).astype(jnp.float32).reshape(seq_len, HY_ORDER, 2, D_HY)
    deltas = jnp.abs(jnp.linspace(HY_MIN_DECAY, HY_MAX_DECAY, D_HY, dtype=jnp.float32))
    decay = jnp.exp(-t * deltas)
    return hf * decay[:, None, None, :]


def hyena_reversed_taps(seq_len, filter_params):
    filt = hyena_filters(seq_len, *filter_params)
    out = []
    for o in range(HY_ORDER):
        h_fwd, h_bwd = filt[:, o, 0], filt[:, o, 1]
        k = jnp.concatenate([h_fwd, jnp.zeros_like(h_fwd[:1]), h_bwd[:0:-1]], axis=0)
        k = k / jnp.sum(jnp.abs(k), axis=0, keepdims=True)
        idx = (seq_len - jnp.arange(2 * seq_len)) % (2 * seq_len)
        out.append(k[idx].T)
    return jnp.stack(out)


def hyena_mixer(hy, conv_wb, hy_bias, fr, *, ch, row_len, pad_batch):
    bsz, seq_len, width = hy.shape
    na = seq_len // ch
    nb = pad_batch
    ht = hy.reshape(bsz, na, ch, width).transpose(3, 2, 1, 0)
    if nb != bsz:
        ht = jnp.pad(ht, ((0, 0), (0, 0), (0, 0), (0, nb - bsz)))
    out = hyena_call(conv_wb, hy_bias, ht.reshape(width, ch, na * nb), fr, ch=ch, na=na, nb=nb, row_len=row_len)
    out = out.reshape(D_HY, ch, na, nb)[..., :bsz]
    return out.transpose(3, 2, 1, 0).reshape(bsz, seq_len, D_HY)


def moe(h2, logits, w1p, b1g, b1l, w2, b2):
    n_tok, d = h2.shape
    i32 = jnp.int32
    top_val, top_idx = lax.top_k(logits, TOP_K)
    weights = jax.nn.softmax(top_val, axis=-1)
    flat_e = top_idx.reshape(-1).astype(i32)
    n_assign = flat_e.shape[0]
    n_blocks = -(-n_assign // MOE_ROWS) + N_EXPERTS
    iota = jnp.arange(n_assign, dtype=i32)
    experts = jnp.arange(N_EXPERTS, dtype=i32)
    e_sorted, order = lax.sort((flat_e, iota), num_keys=1, is_stable=True)
    counts = jnp.sum(flat_e[:, None] == experts[None, :], axis=0).astype(i32)
    padded = (counts + MOE_ROWS - 1) // MOE_ROWS * MOE_ROWS
    end_pad = jnp.cumsum(padded)
    start_pad = end_pad - padded
    start = jnp.cumsum(counts) - counts
    block_start = jnp.arange(n_blocks, dtype=i32) * MOE_ROWS
    block_e = jnp.minimum(jnp.sum(block_start[:, None] >= end_pad[None, :], axis=1), N_EXPERTS - 1).astype(i32)
    off = (block_start - start_pad[block_e])[:, None] + jnp.arange(MOE_ROWS, dtype=i32)[None, :]
    src = jnp.where(off < counts[block_e][:, None], start[block_e][:, None] + off, 0).reshape(-1)
    slot_tok = order[src] // TOP_K
    shift = jnp.sum(jnp.where(e_sorted[:, None] == experts[None, :], (start_pad - start)[None, :], 0), axis=1)
    _, pos = lax.sort((order, iota + shift), num_keys=1)
    n_used = (end_pad[-1] // MOE_ROWS).astype(i32)
    part = n_blocks // MOE_SPLIT
    outs = []
    for s in range(MOE_SPLIT):
        rows = slice(s * part * MOE_ROWS, (s + 1) * part * MOE_ROWS)
        used = jnp.clip(n_used - s * part, 0, part).reshape(1)
        outs.append(moe_expert_blocks(block_e[s * part:(s + 1) * part], used, h2[slot_tok[rows]],
                                      w1p, b1g, b1l, w2, b2))
    out = jnp.concatenate(outs, axis=0)
    return out[pos].reshape(n_tok, TOP_K * d), weights


def _combine_kernel(p_ref, w_ref, x_ref, g2_ref, *rest, final):
    o_ref = rest[-1]
    d = D_MODEL
    w = w_ref[...]
    f = w[:, 0:1] * p_ref[:, 0:d].astype(jnp.float32)
    for k in range(1, TOP_K):
        f = f + w[:, k:k + 1] * p_ref[:, k * d:(k + 1) * d].astype(jnp.float32)
    xn = x_ref[...] + g2_ref[0] * f
    if final:
        ms = jnp.mean(xn * xn, axis=-1, keepdims=True)
        xn = xn * lax.rsqrt(ms + NORM_EPS) * rest[0][...]
    o_ref[...] = xn


def combine_residual(picked, weights, x2d, g2, first_row, rows_per_mod, final_gain=None):
    rows, d = x2d.shape
    tm = MERGE_TM
    blocks_per_mod = rows_per_mod // tm
    first = first_row // tm
    in_specs = [pl.BlockSpec((tm, TOP_K * d), lambda i: (first + i, 0)),
                pl.BlockSpec((tm, TOP_K), lambda i: (first + i, 0)),
                pl.BlockSpec((tm, d), lambda i: (i, 0)),
                pl.BlockSpec((1, 1, d), lambda i: (i // blocks_per_mod, 0, 0))]
    args = [picked, weights, x2d, g2]
    if final_gain is not None:
        in_specs.append(pl.BlockSpec((1, d), lambda i: (0, 0)))
        args.append(final_gain)
    return pl.pallas_call(
        functools.partial(_combine_kernel, final=final_gain is not None),
        grid=(rows // tm,),
        in_specs=in_specs,
        out_specs=pl.BlockSpec((tm, d), lambda i: (i, 0)),
        out_shape=jax.ShapeDtypeStruct((rows, d), jnp.float32),
        compiler_params=pltpu.CompilerParams(
            dimension_semantics=("parallel",),
            vmem_limit_bytes=VMEM_LIMIT_BYTES),
        name="combine_residual",
    )(*args)


def kernel(x, c, ctx, c_ctx, ada_w, ada_b, norm1_g, norm2_g, final_norm_g, w_in,
           hy_conv_w, hy_conv_b, hy_f_w1, hy_f_b1, hy_f_w2, hy_f_b2, hy_f_w3, hy_f_b3, hy_sin_freq, hy_bias,
           s5_lam_re, s5_lam_im, s5_log_dt, s5_b_re, s5_b_im, s5_c_re, s5_c_im, s5_d, s5_glu_w,
           ml_conv_w, ml_conv_b, ml_gate_b, ml_norm_g,
           w_br_hy, w_br_s5, w_br_ml, w_out,
           moe_router_w, moe_router_b, moe_w1, moe_b1, moe_w2, moe_b2):
    bsz, seq_len, d = x.shape
    ctx_len = ctx.shape[1]
    bf, f32 = jnp.bfloat16, jnp.float32
    silu_c = jax.nn.silu(c.astype(f32))
    silu_cc = jax.nn.silu(c_ctx.astype(f32))[None]
    x2 = x.reshape(bsz * seq_len, d)
    c2 = ctx.reshape(bsz * ctx_len, d)
    for l in range(DEPTH):
        need_ctx = l < DEPTH - 1
        mod_l = (silu_c @ ada_w[l] + ada_b[l])[:, None, :]
        mod_c = (silu_cc @ ada_w[l] + ada_b[l])[:, None, :]
        sh1_l, sc1_l, g1_l, sh2_l, sc2_l, g2_l = jnp.split(mod_l, 6, axis=-1)
        sh1_c, sc1_c, g1_c, sh2_c, sc2_c, g2_c = jnp.split(mod_c, 6, axis=-1)

        w_in_p = permute_w_in(w_in[l])
        gain1 = norm1_g[l].reshape(1, d)
        s5_l, qk_l, v_l, hy_l, o_l, mg_l, gt_l = norm_mod_project(x2, sh1_l, sc1_l, gain1, w_in_p, seq_len)
        s5_c, qk_c, v_c, hy_c, o_c, mg_c, gt_c = norm_mod_project(c2, sh1_c, sc1_c, gain1, w_in_p, bsz * ctx_len)
        per_b = lambda a, n: a.reshape(bsz, n, a.shape[-1])

        hy_params = (hy_f_w1[l], hy_f_b1[l], hy_f_w2[l], hy_f_b2[l], hy_f_w3[l], hy_f_b3[l], hy_sin_freq[l])
        conv_wb = jnp.concatenate([hy_conv_w[l], hy_conv_b[l][None]], axis=0)
        hy_out_l = hyena_mixer(per_b(hy_l, seq_len), conv_wb, hy_bias[l], hyena_reversed_taps(seq_len, hy_params),
                               ch=HY_CH, row_len=GRID_W, pad_batch=bsz)
        s5_w = s5_chunk_weights(s5_lam_re[l], s5_lam_im[l], s5_log_dt[l], s5_b_re[l], s5_b_im[l],
                                s5_c_re[l], s5_c_im[l], s5_d[l])
        s5_out_l, s5_out_c = s5_scan(per_b(s5_l, seq_len), per_b(s5_c, ctx_len), s5_w)
        ml_cw = jnp.concatenate([pad_heads(ml_conv_w[l][:, :D_ML]), pad_heads(ml_conv_w[l][:, D_ML:])], axis=1)
        ml_cb = jnp.concatenate([pad_heads(ml_conv_b[l][:D_ML]), pad_heads(ml_conv_b[l][D_ML:])])[None]
        ml_gb = jnp.pad(ml_gate_b[l], (0, GATE_PAD - 4 * ML_HEADS))[None]
        ml_out_c, ml_out_l = mlstm_call(
            per_b(qk_c, ctx_len), per_b(qk_l, seq_len), per_b(v_c, ctx_len), per_b(v_l, seq_len),
            per_b(gt_c, ctx_len), per_b(gt_l, seq_len), per_b(o_c, ctx_len), per_b(o_l, seq_len),
            ml_cw, ml_cb, ml_gb, pad_heads(ml_norm_g[l])[None], row_len=GRID_W)

        wglu, wh, ws, wo = s5_glu_w[l].astype(bf), w_br_hy[l].astype(bf), w_br_s5[l].astype(bf), w_out[l].astype(bf)
        wm = pad_heads(w_br_ml[l].T).T.astype(bf)
        rw = jnp.pad(moe_router_w[l], ((0, 0), (0, ROUTER_PAD - N_EXPERTS))).astype(bf)
        rb = jnp.pad(moe_router_b[l], (0, ROUTER_PAD - N_EXPERTS))[None]
        n2 = norm2_g[l].reshape(1, d)
        flat = lambda a: a.reshape(-1, a.shape[-1])
        x2, h2_l, lg_l = merge_project_residual(
            flat(hy_out_l), flat(s5_out_l), flat(ml_out_l), mg_l, x2, g1_l, sh2_l, sc2_l, n2,
            wglu, wh, ws, wm, wo, rw, rb, seq_len)

        w1p = deinterleave_cast(moe_w1.reshape(DEPTH * N_EXPERTS * d, 2 * D_FF_EXPERT), l)
        moe_w = (w1p.reshape(N_EXPERTS, d, 2 * D_FF_EXPERT),
                 moe_b1[l][:, None, 0::2], moe_b1[l][:, None, 1::2],
                 moe_w2[l].astype(bf), moe_b2[l][:, None, :])
        if need_ctx:
            hy_out_c = hyena_mixer(per_b(hy_c, ctx_len), conv_wb, hy_bias[l],
                                   hyena_reversed_taps(ctx_len, hy_params),
                                   ch=ctx_len, row_len=ctx_len, pad_batch=LANES)
            c2, h2_c, lg_c = merge_project_residual(
                flat(hy_out_c), flat(s5_out_c), flat(ml_out_c), mg_c, c2, g1_c, sh2_c, sc2_c, n2,
                wglu, wh, ws, wm, wo, rw, rb, bsz * ctx_len)
            tok = jnp.concatenate([h2_c, h2_l], axis=0)
            lg = jnp.concatenate([lg_c, lg_l], axis=0)
            picked, wts = moe(tok, lg[:, :N_EXPERTS], *moe_w)
            c2 = combine_residual(picked, wts, c2, g2_c, 0, bsz * ctx_len)
            x2 = combine_residual(picked, wts, x2, g2_l, bsz * ctx_len, seq_len)
        else:
            picked, wts = moe(h2_l, lg_l[:, :N_EXPERTS], *moe_w)
            x2 = combine_residual(picked, wts, x2, g2_l, 0, seq_len, final_gain=final_norm_g.reshape(1, d))
    return x2.reshape(bsz, seq_len, d)
```

```python
import functools
import math

import jax
import jax.numpy as jnp
from jax import lax
from jax.experimental import pallas as pl
from jax.experimental.pallas import tpu as pltpu

D_MODEL = 1024
DEPTH = 4
GRID_W = 64

D_HY = 384
D_S5 = 384
D_ML = 384
N_BRANCH = 3
SHORT_CONV = 3

HY_ORDER = 2
HY_EMB = 33
HY_BANDS = (HY_EMB - 1) // 2
HY_DECAY_TARGET = 1e-2
HY_FAST_DECAY_PCT = 0.3
HY_SLOW_DECAY_PCT = 1.5
HY_MIN_DECAY = math.log(HY_DECAY_TARGET) / HY_SLOW_DECAY_PCT
HY_MAX_DECAY = math.log(HY_DECAY_TARGET) / HY_FAST_DECAY_PCT

S5_GROUP = 16
S5_GROUPS = D_S5 // S5_GROUP
S5_STATE = 64

ML_HEADS = 4
ML_HEAD_DIM = D_ML // ML_HEADS
NEG = -1e30

N_EXPERTS = 32
TOP_K = 4
D_FF_EXPERT = 512
SWIGLU_LIMIT = 7.0
SWIGLU_ALPHA = 1.702

NORM_EPS = 1e-6

COL_SIZES = (D_S5, 2 * D_ML, D_ML, 4 * ML_HEADS, (1 + HY_ORDER) * D_HY, D_ML, N_BRANCH * D_MODEL)

LANES = 128
VMEM_LIMIT_BYTES = 56 * 1024 * 1024

ML_HP = LANES
ML_DP = ML_HEADS * ML_HP
ML_TC = 256
GATE_PAD = LANES

PROJ_OUT = (("s5", D_S5, jnp.bfloat16), ("qk", 2 * ML_DP, jnp.bfloat16), ("v", ML_DP, jnp.bfloat16),
            ("hy", (1 + HY_ORDER) * D_HY, jnp.bfloat16), ("o", ML_DP, jnp.bfloat16),
            ("mg", N_BRANCH * D_MODEL, jnp.bfloat16), ("gt", GATE_PAD, jnp.float32))
PROJ_COLS = sum(w for _, w, _ in PROJ_OUT)
PROJ_TM = 512

MOE_ROWS = 512
MOE_SPLIT = 1
MERGE_TM = 512
ROUTER_PAD = LANES

HY_CB = 8
HY_CH = 128


def _proj_kernel(x_ref, shift_ref, scale_ref, g_ref, w_ref, *o_refs):
    x = x_ref[...]
    ms = jnp.mean(x * x, axis=-1, keepdims=True)
    y = x * lax.rsqrt(ms + NORM_EPS) * g_ref[...]
    h = (y * (1.0 + scale_ref[0]) + shift_ref[0]).astype(jnp.bfloat16)
    start = 0
    for o_ref, (_, width, _) in zip(o_refs, PROJ_OUT):
        o_ref[...] = jnp.dot(h, w_ref[:, start:start + width],
                             preferred_element_type=jnp.float32).astype(o_ref.dtype)
        start += width


def norm_mod_project(x2d, shift, scale, gain, w_bf16, rows_per_mod):
    rows, d = x2d.shape
    tm = PROJ_TM
    blocks_per_mod = rows_per_mod // tm
    return pl.pallas_call(
        _proj_kernel,
        grid=(rows // tm,),
        in_specs=[
            pl.BlockSpec((tm, d), lambda i: (i, 0)),
            pl.BlockSpec((1, 1, d), lambda i: (i // blocks_per_mod, 0, 0)),
            pl.BlockSpec((1, 1, d), lambda i: (i // blocks_per_mod, 0, 0)),
            pl.BlockSpec((1, d), lambda i: (0, 0)),
            pl.BlockSpec((d, PROJ_COLS), lambda i: (0, 0)),
        ],
        out_specs=[pl.BlockSpec((tm, w), lambda i: (i, 0)) for _, w, _ in PROJ_OUT],
        out_shape=[jax.ShapeDtypeStruct((rows, w), dt) for _, w, dt in PROJ_OUT],
        compiler_params=pltpu.CompilerParams(
            dimension_semantics=("parallel",),
            vmem_limit_bytes=VMEM_LIMIT_BYTES),
        name="norm_mod_project",
    )(x2d, shift, scale, gain, w_bf16)


def pad_heads(w):
    lead = w.shape[:-1]
    w = w.reshape(lead + (ML_HEADS, ML_HEAD_DIM))
    w = jnp.pad(w, [(0, 0)] * len(lead) + [(0, 0), (0, ML_HP - ML_HEAD_DIM)])
    return w.reshape(lead + (ML_DP,))


def permute_w_in(w_in):
    cols, start = [], 0
    for s in COL_SIZES:
        cols.append(w_in[:, start:start + s])
        start += s
    s5, qk, v, gt, hy, o, mg = cols
    qk = jnp.concatenate([pad_heads(qk[:, :D_ML]), pad_heads(qk[:, D_ML:])], axis=1)
    gt = jnp.pad(gt, ((0, 0), (0, GATE_PAD - gt.shape[1])))
    return jnp.concatenate([s5, qk, pad_heads(v), hy, pad_heads(o), mg, gt], axis=1).astype(jnp.bfloat16)


def _merge_kernel(hy_ref, s5_ref, ml_ref, mg_ref, x_ref, g1_ref, sh2_ref, sc2_ref, n2_ref,
                  wglu_ref, wh_ref, ws_ref, wm_ref, wo_ref, rw_ref, rb_ref,
                  xo_ref, h2_ref, lg_ref):
    bf = jnp.bfloat16
    f32 = jnp.float32
    d = D_MODEL
    g = jax.nn.gelu(s5_ref[...].astype(f32))
    s5 = g * jax.nn.sigmoid(jnp.dot(g.astype(bf), wglu_ref[...], preferred_element_type=f32))
    y = jax.nn.sigmoid(mg_ref[:, 0:d].astype(f32)) * jnp.dot(hy_ref[...], wh_ref[...], preferred_element_type=f32)
    y = y + jax.nn.sigmoid(mg_ref[:, d:2 * d].astype(f32)) * jnp.dot(s5.astype(bf), ws_ref[...],
                                                                     preferred_element_type=f32)
    y = y + jax.nn.sigmoid(mg_ref[:, 2 * d:3 * d].astype(f32)) * jnp.dot(ml_ref[...], wm_ref[...],
                                                                         preferred_element_type=f32)
    out = jnp.dot(y.astype(bf), wo_ref[...], preferred_element_type=f32)
    xn = x_ref[...] + g1_ref[0] * out
    xo_ref[...] = xn
    ms = jnp.mean(xn * xn, axis=-1, keepdims=True)
    h2 = (xn * lax.rsqrt(ms + NORM_EPS) * n2_ref[...]) * (1.0 + sc2_ref[0]) + sh2_ref[0]
    h2b = h2.astype(bf)
    h2_ref[...] = h2b
    lg_ref[...] = jnp.dot(h2b, rw_ref[...], preferred_element_type=f32) + rb_ref[...]


def merge_project_residual(hy, s5, ml, mg, x2d, g1, sh2, sc2, n2, wglu, wh, ws, wm, wo, rw, rb, rows_per_mod):
    rows, d = x2d.shape
    tm = MERGE_TM
    blocks_per_mod = rows_per_mod // tm
    row_spec = lambda c: pl.BlockSpec((tm, c), lambda i: (i, 0))
    mod_spec = pl.BlockSpec((1, 1, d), lambda i: (i // blocks_per_mod, 0, 0))
    full = lambda a: pl.BlockSpec(a.shape, lambda i: (0, 0))
    return pl.pallas_call(
        _merge_kernel,
        grid=(rows // tm,),
        in_specs=[row_spec(D_HY), row_spec(D_S5), row_spec(ML_DP), row_spec(N_BRANCH * d), row_spec(d),
                  mod_spec, mod_spec, mod_spec, full(n2),
                  full(wglu), full(wh), full(ws), full(wm), full(wo), full(rw), full(rb)],
        out_specs=[row_spec(d), row_spec(d), row_spec(ROUTER_PAD)],
        out_shape=[jax.ShapeDtypeStruct((rows, d), jnp.float32),
                   jax.ShapeDtypeStruct((rows, d), jnp.bfloat16),
                   jax.ShapeDtypeStruct((rows, ROUTER_PAD), jnp.float32)],
        compiler_params=pltpu.CompilerParams(
            dimension_semantics=("parallel",),
            vmem_limit_bytes=VMEM_LIMIT_BYTES),
        name="merge_project_residual",
    )(hy, s5, ml, mg, x2d, g1, sh2, sc2, n2, wglu, wh, ws, wm, wo, rw, rb)


def _deinterleave_kernel(w_ref, p_ref, o_ref):
    o_ref[...] = jnp.dot(w_ref[...].astype(jnp.bfloat16), p_ref[...],
                         preferred_element_type=jnp.float32).astype(jnp.bfloat16)


def deinterleave_cast(w_all, layer):
    rows, n = w_all.shape[0] // DEPTH, w_all.shape[1]
    tm = 1024
    first = layer * (rows // tm)
    src = jnp.concatenate([jnp.arange(0, n, 2), jnp.arange(1, n, 2)])
    perm = (jnp.arange(n)[:, None] == src[None, :]).astype(jnp.bfloat16)
    return pl.pallas_call(
        _deinterleave_kernel,
        grid=(rows // tm,),
        in_specs=[pl.BlockSpec((tm, n), lambda i: (first + i, 0)), pl.BlockSpec((n, n), lambda i: (0, 0))],
        out_specs=pl.BlockSpec((tm, n), lambda i: (i, 0)),
        out_shape=jax.ShapeDtypeStruct((rows, n), jnp.bfloat16),
        compiler_params=pltpu.CompilerParams(
            dimension_semantics=("parallel",),
            vmem_limit_bytes=VMEM_LIMIT_BYTES),
        name="deinterleave_cast",
    )(w_all, perm)


def _moe_kernel(be_ref, nu_ref, x_ref, w1_ref, b1g_ref, b1l_ref, w2_ref, b2_ref, o_ref):
    f = D_FF_EXPERT

    @pl.when(pl.program_id(0) < nu_ref[0])
    def _():
        x = x_ref[...]
        hg = jnp.dot(x, w1_ref[0, :, 0:f], preferred_element_type=jnp.float32) + b1g_ref[0]
        hl = jnp.dot(x, w1_ref[0, :, f:2 * f], preferred_element_type=jnp.float32) + b1l_ref[0]
        x_glu = jnp.minimum(hg, SWIGLU_LIMIT)
        x_lin = jnp.clip(hl, -SWIGLU_LIMIT, SWIGLU_LIMIT)
        act = x_glu * jax.nn.sigmoid(SWIGLU_ALPHA * x_glu) * (x_lin + 1.0)
        out = jnp.dot(act.astype(jnp.bfloat16), w2_ref[0], preferred_element_type=jnp.float32) + b2_ref[0]
        o_ref[...] = out.astype(o_ref.dtype)


def moe_expert_blocks(block_e, n_used, x_sorted, w1p, b1g, b1l, w2, b2):
    n_slots, d = x_sorted.shape
    n_blocks = n_slots // MOE_ROWS
    f = D_FF_EXPERT

    def row_map(i, be, nu):
        return (jnp.maximum(jnp.minimum(i, nu[0] - 1), 0), 0)

    def w_map(i, be, nu):
        return (be[jnp.maximum(jnp.minimum(i, nu[0] - 1), 0)], 0, 0)

    grid_spec = pltpu.PrefetchScalarGridSpec(
        num_scalar_prefetch=2,
        grid=(n_blocks,),
        in_specs=[
            pl.BlockSpec((MOE_ROWS, d), row_map),
            pl.BlockSpec((1, d, 2 * f), w_map),
            pl.BlockSpec((1, 1, f), w_map),
            pl.BlockSpec((1, 1, f), w_map),
            pl.BlockSpec((1, f, d), w_map),
            pl.BlockSpec((1, 1, d), w_map),
        ],
        out_specs=pl.BlockSpec((MOE_ROWS, d), row_map),
    )
    return pl.pallas_call(
        _moe_kernel,
        grid_spec=grid_spec,
        out_shape=jax.ShapeDtypeStruct((n_slots, d), jnp.bfloat16),
        compiler_params=pltpu.CompilerParams(
            dimension_semantics=("arbitrary",),
            vmem_limit_bytes=VMEM_LIMIT_BYTES),
        name="moe_expert_blocks",
    )(block_e, n_used, x_sorted, w1p, b1g, b1l, w2, b2)


S5_T = 16
S5_W = S5_T * S5_GROUP
S5_HALF = 2 * S5_STATE


def _s5_kernel(u_ref, wp_ref, wm_ref, wq0_ref, wq1_ref, ar_ref, ai_ref, y_ref, s_scr, x_scr, yb_scr, ur_scr,
               *, n_batch, ctx_chunks, lat_chunks):
    nb = n_batch
    u = u_ref[0]
    segments = ((0, ctx_chunks), (ctx_chunks * nb, lat_chunks))
    for seg_start, seg_chunks in segments:
        for i in range(seg_chunks):
            dst = seg_start + nb * i
            src = seg_start + nb * (seg_chunks - 1 - i)
            ur_scr[dst:dst + nb, :] = u_ref[0, src:src + nb, :]
    s_scr[...] = (jnp.dot(u, wp_ref[0, 0:S5_W, :], preferred_element_type=jnp.float32)
                  + jnp.dot(ur_scr[...], wp_ref[0, S5_W:2 * S5_W, :], preferred_element_type=jnp.float32))
    ar = jnp.broadcast_to(ar_ref[0], (nb, S5_HALF))
    ai = jnp.broadcast_to(ai_ref[0], (nb, S5_HALF))

    def step(i, carry):
        xr, xi = carry
        r = pl.multiple_of(i * nb, nb)
        x_scr[pl.ds(r, nb), 0:S5_HALF] = xr
        x_scr[pl.ds(r, nb), S5_HALF:2 * S5_HALF] = xi
        sr = s_scr[pl.ds(r, nb), 0:S5_HALF]
        si = s_scr[pl.ds(r, nb), S5_HALF:2 * S5_HALF]
        return ar * xr - ai * xi + sr, ar * xi + ai * xr + si

    zero = jnp.zeros((nb, S5_HALF), jnp.float32)
    lax.fori_loop(0, ctx_chunks + lat_chunks, step, (zero, zero))

    xin = x_scr[...].astype(jnp.bfloat16)
    s_scr[...] = (jnp.dot(u, wm_ref[0], preferred_element_type=jnp.float32)
                  + jnp.dot(xin, wq0_ref[0], preferred_element_type=jnp.float32))
    yb_scr[...] = jnp.dot(xin, wq1_ref[0], preferred_element_type=jnp.float32)
    for seg_start, seg_chunks in segments:
        for i in range(seg_chunks):
            dst = seg_start + nb * i
            src = seg_start + nb * (seg_chunks - 1 - i)
            y_ref[0, dst:dst + nb, :] = (s_scr[dst:dst + nb, :] + yb_scr[src:src + nb, :]).astype(y_ref.dtype)


def s5_scan_call(u, wp, wm, wq0, wq1, ar, ai, *, n_batch, ctx_chunks):
    groups, rows, _ = u.shape
    lat_chunks = rows // n_batch - ctx_chunks
    per_group = lambda a: pl.BlockSpec((1,) + a.shape[1:], lambda g: (g, 0, 0))
    return pl.pallas_call(
        functools.partial(_s5_kernel, n_batch=n_batch, ctx_chunks=ctx_chunks, lat_chunks=lat_chunks),
        grid=(groups,),
        in_specs=[per_group(a) for a in (u, wp, wm, wq0, wq1, ar, ai)],
        out_specs=pl.BlockSpec((1, rows, S5_W), lambda g: (g, 0, 0)),
        out_shape=jax.ShapeDtypeStruct((groups, rows, S5_W), jnp.bfloat16),
        scratch_shapes=[pltpu.VMEM((rows, S5_W), jnp.float32)] * 3 + [pltpu.VMEM((rows, S5_W), jnp.bfloat16)],
        compiler_params=pltpu.CompilerParams(
            dimension_semantics=("parallel",),
            vmem_limit_bytes=VMEM_LIMIT_BYTES),
        name="s5_scan",
    )(u, wp, wm, wq0, wq1, ar, ai)


def s5_chunk_weights(lam_re, lam_im, log_dt, b_re, b_im, c_re, c_im, d_skip):
    f32 = jnp.float32
    t_len, g_n, p_n, n_n = S5_T, S5_GROUPS, S5_STATE, S5_GROUP
    b_mat = lax.complex(b_re.astype(f32), b_im.astype(f32))
    c_mat = lax.complex(c_re.astype(f32), c_im.astype(f32))
    lam = lax.complex(lam_re.astype(f32), lam_im.astype(f32))
    lam_dt = lam * jnp.exp(log_dt.astype(f32))[..., None]
    b_bar = ((jnp.exp(lam_dt) - 1.0) / lam)[..., None] * b_mat
    j = jnp.arange(t_len + 1, dtype=f32)
    pw = jnp.exp(j[:, None, None, None] * lam_dt[None])

    kern = [jnp.real(jnp.einsum('gnp,jgp,gpm->jgnm', c_mat, pw[:t_len, d], b_bar[d])) for d in range(2)]
    idx = jnp.arange(t_len)
    diff = idx[None, :] - idx[:, None]
    k0 = kern[0][jnp.clip(diff, 0, None)]
    k1 = kern[1][jnp.clip(-diff, 0, None)]
    m5 = (jnp.where((diff >= 0)[:, :, None, None, None], k0, 0.0)
          + jnp.where((diff <= 0)[:, :, None, None, None], k1, 0.0))
    skip = jnp.eye(t_len, dtype=f32)[:, :, None, None, None] * (
        d_skip.astype(f32).reshape(g_n, n_n)[None, None, :, :, None] * jnp.eye(n_n, dtype=f32)[None, None, None])
    wm = (m5 + skip).transpose(2, 0, 4, 1, 3).reshape(g_n, S5_W, S5_W)

    pf = jnp.einsum('sgp,gpm->gsmp', pw[t_len - 1 - idx, 0], b_bar[0]).reshape(g_n, S5_W, p_n)
    pb = jnp.einsum('sgp,gpm->gsmp', pw[idx, 1], b_bar[1]).reshape(g_n, S5_W, p_n)
    z = jnp.zeros_like(jnp.real(pf))
    wp = jnp.concatenate([
        jnp.concatenate([jnp.real(pf), z, jnp.imag(pf), z], axis=-1),
        jnp.concatenate([z, jnp.real(pb), z, jnp.imag(pb)], axis=-1)], axis=1)

    q0 = jnp.einsum('gnp,tgp->gptn', c_mat, pw[idx + 1, 0]).reshape(g_n, p_n, S5_W)
    q1 = jnp.einsum('gnp,tgp->gptn', c_mat, pw[t_len - idx, 1]).reshape(g_n, p_n, S5_W)
    zq = jnp.zeros_like(jnp.real(q0))
    wq0 = jnp.concatenate([jnp.real(q0), zq, -jnp.imag(q0), zq], axis=1)
    wq1 = jnp.concatenate([zq, jnp.real(q1), zq, -jnp.imag(q1)], axis=1)
    lam_t = pw[t_len]
    ar = jnp.concatenate([jnp.real(lam_t[0]), jnp.real(lam_t[1])], axis=-1)[:, None, :]
    ai = jnp.concatenate([jnp.imag(lam_t[0]), jnp.imag(lam_t[1])], axis=-1)[:, None, :]
    bf = jnp.bfloat16
    return wp.astype(bf), wm.astype(bf), wq0.astype(bf), wq1.astype(bf), ar, ai


def s5_scan(u_l, u_c, weights):
    bsz, seq_len, _ = u_l.shape
    ctx_len = u_c.shape[1]

    def to_chunks(a):
        a = a.reshape(bsz, a.shape[1] // S5_T, S5_T, S5_GROUPS, S5_GROUP).transpose(3, 1, 0, 2, 4)
        return a.reshape(S5_GROUPS, -1, S5_W)

    u = jnp.concatenate([to_chunks(u_c), to_chunks(u_l)], axis=1).astype(jnp.bfloat16)
    y = s5_scan_call(u, *weights, n_batch=bsz, ctx_chunks=ctx_len // S5_T)

    def from_chunks(a, length):
        a = a.reshape(S5_GROUPS, length // S5_T, bsz, S5_T, S5_GROUP)
        return a.transpose(2, 1, 3, 0, 4).reshape(bsz, length, D_S5)

    ctx_rows = ctx_len // S5_T * bsz
    return from_chunks(y[:, ctx_rows:], seq_len), from_chunks(y[:, :ctx_rows], ctx_len)


def _dot01(a01, x):
    bf, f32 = jnp.bfloat16, jnp.float32
    hi = x.astype(bf)
    r1 = x - hi.astype(f32)
    mid = r1.astype(bf)
    lo = (r1 - mid.astype(f32)).astype(bf)
    d = lambda y: jnp.dot(a01, y, preferred_element_type=f32)
    return d(hi) + d(mid) + d(lo)


def _mlstm_kernel(qkc_ref, qkl_ref, vc_ref, vl_ref, gc_ref, gl_ref, oc_ref, ol_ref,
                  cw_ref, cb_ref, gb_ref, ng_ref,
                  outc_ref, outl_ref,
                  q_scr, k_scr, v_scr, g_scr, h_scr, c_scr, n_scr, m_scr,
                  *, ctx_len, seq_len, row_len):
    bf, f32 = jnp.bfloat16, jnp.float32
    tc = ML_TC
    nc_ctx = ctx_len // tc
    nc_lat = seq_len // tc
    nc_tot = nc_ctx + nc_lat

    def conv_silu(x, n_rows, rlen):
        row = lax.broadcasted_iota(jnp.int32, (n_rows, 1), 0) % rlen
        prev = jnp.where(row == 0, 0.0, pltpu.roll(x, 1, 0))
        nxt = jnp.where(row == rlen - 1, 0.0, pltpu.roll(x, n_rows - 1, 0))
        y = cb_ref[...] + prev * cw_ref[0:1, :] + x * cw_ref[1:2, :] + nxt * cw_ref[2:3, :]
        return y * jax.nn.sigmoid(y)

    kscale = ML_HEAD_DIM ** -0.5
    a = conv_silu(qkc_ref[0].astype(f32), ctx_len, ctx_len)
    q_scr[0:ctx_len, :] = a[:, 0:ML_DP].astype(bf)
    k_scr[0:ctx_len, :] = (a[:, ML_DP:2 * ML_DP] * kscale).astype(bf)
    v_scr[0:ctx_len, :] = vc_ref[0]
    g_scr[0:ctx_len, :] = gc_ref[0] + gb_ref[...]

    def prep(c, _):
        r = pl.multiple_of(c * tc, tc)
        a = conv_silu(qkl_ref[0, pl.ds(r, tc), :].astype(f32), tc, row_len)
        q_scr[pl.ds(ctx_len + r, tc), :] = a[:, 0:ML_DP].astype(bf)
        k_scr[pl.ds(ctx_len + r, tc), :] = (a[:, ML_DP:2 * ML_DP] * kscale).astype(bf)
        v_scr[pl.ds(ctx_len + r, tc), :] = vl_ref[0, pl.ds(r, tc), :]
        g_scr[pl.ds(ctx_len + r, tc), :] = gl_ref[0, pl.ds(r, tc), :] + gb_ref[...]
        return 0

    lax.fori_loop(0, nc_lat, prep, 0)

    ti = lax.broadcasted_iota(jnp.int32, (tc, tc), 0)
    si = lax.broadcasted_iota(jnp.int32, (tc, tc), 1)

    for d in range(2):
        causal = (si <= ti) if d == 0 else (si >= ti)
        tri = causal.astype(bf)
        c_scr[...] = jnp.zeros_like(c_scr)
        n_scr[...] = jnp.zeros_like(n_scr)
        m_scr[...] = jnp.full_like(m_scr, NEG)

        def chunk(i, _, d=d, causal=causal, tri=tri):
            if d == 0:
                ci = i
            else:
                ci = jnp.where(i < nc_ctx, nc_ctx - 1 - i, nc_tot + nc_ctx - 1 - i)
            r0 = pl.multiple_of(ci * tc, tc)
            gts = g_scr[pl.ds(r0, tc), :]
            logf = jax.nn.log_sigmoid(gts)
            bm = _dot01(tri, logf)
            gts_t = gts.T
            bm_t = bm.T
            b_end_row = bm[tc - 1:tc, :] if d == 0 else bm[0:1, :]
            for h in range(ML_HEADS):
                icol = h + 8 * d
                fcol = ML_HEADS + h + 8 * d
                lanes = slice(h * ML_HP, (h + 1) * ML_HP)
                i_col = gts[:, icol:icol + 1]
                b_col = bm[:, fcol:fcol + 1]
                i_row = gts_t[icol:icol + 1, :]
                b_row = bm_t[fcol:fcol + 1, :]
                b_end = b_end_row[:, fcol:fcol + 1]
                m_prev = m_scr[h:h + 1, 0:1]
                dmat = jnp.where(causal, b_col - b_row + i_row, NEG)
                a_col = b_col + m_prev
                m_out = jnp.maximum(a_col, jnp.max(dmat, axis=1, keepdims=True))
                wmat = jnp.exp(dmat - m_out)
                wa = jnp.exp(a_col - m_out)
                q = q_scr[pl.ds(r0, tc), lanes]
                k = k_scr[pl.ds(r0, tc), lanes]
                v = v_scr[pl.ds(r0, tc), lanes]
                s = lax.dot_general(q, k, (((1,), (1,)), ((), ())), preferred_element_type=f32) * wmat
                c_mem = c_scr[h * ML_HP:(h + 1) * ML_HP, :]
                n_row = n_scr[h:h + 1, :]
                qf = q.astype(f32)
                num = (jnp.dot(s.astype(bf), v, preferred_element_type=f32)
                       + wa * lax.dot_general(q, c_mem.astype(bf), (((1,), (1,)), ((), ())),
                                              preferred_element_type=f32))
                den = jnp.sum(s, axis=1, keepdims=True) + wa * jnp.sum(qf * n_row, axis=1, keepdims=True)
                hval = num / jnp.maximum(jnp.abs(den), jnp.exp(-m_out))
                if d == 0:
                    h_scr[pl.ds(r0, tc), lanes] = hval
                else:
                    h_scr[pl.ds(r0, tc), lanes] += hval
                g_col = b_end - b_col + i_col
                m_new = jnp.maximum(b_end + m_prev, jnp.max(g_col, axis=0, keepdims=True))
                decay = jnp.exp(b_end + m_prev - m_new)
                w_col = jnp.exp(g_col - m_new)
                vw = (v.astype(f32) * w_col).astype(bf)
                c_scr[h * ML_HP:(h + 1) * ML_HP, :] = decay * c_mem + lax.dot_general(
                    vw, k, (((0,), (0,)), ((), ())), preferred_element_type=f32)
                n_scr[h:h + 1, :] = decay * n_row + jnp.sum(k.astype(f32) * w_col, axis=0, keepdims=True)
                m_scr[h:h + 1, :] = jnp.broadcast_to(m_new, (1, LANES))
            return 0

        lax.fori_loop(0, nc_tot, chunk, 0)

    def head_out(hs, o):
        outs = []
        for h in range(ML_HEADS):
            x = hs[:, h * ML_HP:(h + 1) * ML_HP]
            ms = jnp.sum(x * x, axis=1, keepdims=True) * (1.0 / ML_HEAD_DIM)
            outs.append(x * lax.rsqrt(ms + NORM_EPS))
        hn = jnp.concatenate(outs, axis=1) * ng_ref[...]
        return (hn * jax.nn.sigmoid(o.astype(f32))).astype(bf)

    outc_ref[0] = head_out(h_scr[0:ctx_len, :], oc_ref[0])

    def fin(c, _):
        r = pl.multiple_of(c * tc, tc)
        outl_ref[0, pl.ds(r, tc), :] = head_out(h_scr[pl.ds(ctx_len + r, tc), :], ol_ref[0, pl.ds(r, tc), :])
        return 0

    lax.fori_loop(0, nc_lat, fin, 0)


def mlstm_call(qk_c, qk_l, v_c, v_l, g_c, g_l, o_c, o_l, conv_w, conv_b, gate_b, norm_g, *, row_len):
    bsz, ctx_len, _ = qk_c.shape
    seq_len = qk_l.shape[1]
    tot = ctx_len + seq_len
    per_b = lambda a: pl.BlockSpec((1,) + a.shape[1:], lambda b: (b, 0, 0))
    full = lambda a: pl.BlockSpec(a.shape, lambda b: (0, 0))
    return pl.pallas_call(
        functools.partial(_mlstm_kernel, ctx_len=ctx_len, seq_len=seq_len, row_len=row_len),
        grid=(bsz,),
        in_specs=[per_b(a) for a in (qk_c, qk_l, v_c, v_l, g_c, g_l, o_c, o_l)]
        + [full(a) for a in (conv_w, conv_b, gate_b, norm_g)],
        out_specs=[pl.BlockSpec((1, ctx_len, ML_DP), lambda b: (b, 0, 0)),
                   pl.BlockSpec((1, seq_len, ML_DP), lambda b: (b, 0, 0))],
        out_shape=[jax.ShapeDtypeStruct((bsz, ctx_len, ML_DP), jnp.bfloat16),
                   jax.ShapeDtypeStruct((bsz, seq_len, ML_DP), jnp.bfloat16)],
        scratch_shapes=[pltpu.VMEM((tot, ML_DP), jnp.bfloat16)] * 3
        + [pltpu.VMEM((tot, GATE_PAD), jnp.float32), pltpu.VMEM((tot, ML_DP), jnp.float32),
           pltpu.VMEM((ML_DP, ML_HP), jnp.float32), pltpu.VMEM((8, LANES), jnp.float32),
           pltpu.VMEM((8, LANES), jnp.float32)],
        compiler_params=pltpu.CompilerParams(
            dimension_semantics=("parallel",),
            vmem_limit_bytes=VMEM_LIMIT_BYTES),
        name="mlstm",
    )(qk_c, qk_l, v_c, v_l, g_c, g_l, o_c, o_l, conv_w, conv_b, gate_b, norm_g)


def _hyena_kernel(cw_ref, hb_ref, v_ref, x1_ref, x2_ref, fr_ref, o_ref, g_scr, *, ch, na, nb, row_len):
    bf, f32 = jnp.bfloat16, jnp.float32
    nl = na * nb
    cblk = pl.program_id(0)
    row = lax.broadcasted_iota(jnp.int32, (ch, 1), 0) % row_len
    first, last = row == 0, row == row_len - 1
    lane = lax.broadcasted_iota(jnp.int32, (1, nl), 1)
    below = lax.broadcasted_iota(jnp.int32, (ch, ch), 1) < lax.broadcasted_iota(jnp.int32, (ch, ch), 0)

    def short_conv(x_ref, j, part):
        x = x_ref[j].astype(f32)
        cidx = part * D_HY + cblk * HY_CB + j
        prev = jnp.where(first, 0.0, pltpu.roll(x, 1, 0))
        nxt = jnp.where(last, 0.0, pltpu.roll(x, ch - 1, 0))
        return cw_ref[3, cidx] + cw_ref[0, cidx] * prev + cw_ref[1, cidx] * x + cw_ref[2, cidx] * nxt

    def channel(j, _):
        z = short_conv(v_ref, j, 0)
        for o, xg_ref in ((0, x1_ref), (1, x2_ref)):
            gate = short_conv(xg_ref, j, 1 + o)
            taps = fr_ref[o, pl.ds(j, 1), :]
            r_lo = pltpu.roll(jnp.broadcast_to(taps[:, 0:ch], (ch, ch)), 0, 1, stride=1, stride_axis=0)
            for q in range(2 * na - 1):
                r_hi = pltpu.roll(jnp.broadcast_to(taps[:, (q + 1) * ch:(q + 2) * ch], (ch, ch)),
                                  0, 1, stride=1, stride_axis=0)
                g_scr[q * ch:(q + 1) * ch, :] = jnp.where(below, r_lo, r_hi).astype(bf)
                r_lo = r_hi
            zp = jnp.dot(g_scr[...], z.astype(bf), preferred_element_type=f32)
            y = zp[(na - 1) * ch:na * ch]
            for p in range(1, na):
                zpos = zp[(na - 1 - p) * ch:(na - p) * ch]
                zneg = zp[(2 * na - 1 - p) * ch:(2 * na - p) * ch]
                y = y + pltpu.roll(jnp.where(lane < (na - p) * nb, zpos, zneg), p * nb, 1)
            z = gate * (y + hb_ref[o, cblk * HY_CB + j] * z)
        o_ref[j] = z.astype(bf)
        return 0

    lax.fori_loop(0, HY_CB, channel, 0, unroll=2)


def hyena_call(conv_wb, hy_bias, ht, fr, *, ch, na, nb, row_len):
    nl = na * nb
    n_blk = D_HY // HY_CB
    part = lambda k: pl.BlockSpec((HY_CB, ch, nl), lambda i, *_: (i + k * n_blk, 0, 0))
    grid_spec = pltpu.PrefetchScalarGridSpec(
        num_scalar_prefetch=2,
        grid=(n_blk,),
        in_specs=[part(0), part(1), part(2),
                  pl.BlockSpec((HY_ORDER, HY_CB, 2 * ch * na), lambda i, *_: (0, i, 0))],
        out_specs=pl.BlockSpec((HY_CB, ch, nl), lambda i, *_: (i, 0, 0)),
        scratch_shapes=[pltpu.VMEM(((2 * na - 1) * ch, ch), jnp.bfloat16)],
    )
    return pl.pallas_call(
        functools.partial(_hyena_kernel, ch=ch, na=na, nb=nb, row_len=row_len),
        grid_spec=grid_spec,
        out_shape=jax.ShapeDtypeStruct((D_HY, ch, nl), jnp.bfloat16),
        compiler_params=pltpu.CompilerParams(
            dimension_semantics=("parallel",),
            vmem_limit_bytes=VMEM_LIMIT_BYTES),
        name="hyena",
    )(conv_wb, hy_bias, ht, ht, ht, fr)


def hyena_filters(seq_len, w1, b1, w2, b2, w3, b3, sin_freq):
    t = jnp.linspace(0.0, 1.0, seq_len, dtype=jnp.float32)[:, None]
    w = 2.0 * math.pi * jnp.arange(seq_len, dtype=jnp.float32)[:, None] / seq_len
    f = jnp.linspace(1e-4, HY_BANDS - 1, HY_BANDS, dtype=jnp.float32)[None, :]
    z = jnp.concatenate([t, jnp.cos(f * w), -jnp.sin(f * w)], axis=-1)
    hdn = jnp.sin(sin_freq[0] * (z @ w1 + b1))
    hdn = jnp.sin(sin_freq[1] * (hdn @ w2 + b2))
    hf = (hdn @ w3 + b3).astype(jnp.float32).reshape(seq_len, HY_ORDER, 2, D_HY)
    deltas = jnp.abs(jnp.linspace(HY_MIN_DECAY, HY_MAX_DECAY, D_HY, dtype=jnp.float32))
    decay = jnp.exp(-t * deltas)
    return hf * decay[:, None, None, :]


def hyena_reversed_taps(seq_len, filter_params):
    filt = hyena_filters(seq_len, *filter_params)
    out = []
    for o in range(HY_ORDER):
        h_fwd, h_bwd = filt[:, o, 0], filt[:, o, 1]
        k = jnp.concatenate([h_fwd, jnp.zeros_like(h_fwd[:1]), h_bwd[:0:-1]], axis=0)
        k = k / jnp.sum(jnp.abs(k), axis=0, keepdims=True)
        idx = (seq_len - jnp.arange(2 * seq_len)) % (2 * seq_len)
        out.append(k[idx].T)
    return jnp.stack(out)


def hyena_mixer(hy, conv_wb, hy_bias, fr, *, ch, row_len, pad_batch):
    bsz, seq_len, width = hy.shape
    na = seq_len // ch
    nb = pad_batch
    ht = hy.reshape(bsz, na, ch, width).transpose(3, 2, 1, 0)
    if nb != bsz:
        ht = jnp.pad(ht, ((0, 0), (0, 0), (0, 0), (0, nb - bsz)))
    out = hyena_call(conv_wb, hy_bias, ht.reshape(width, ch, na * nb), fr, ch=ch, na=na, nb=nb, row_len=row_len)
    out = out.reshape(D_HY, ch, na, nb)[..., :bsz]
    return out.transpose(3, 2, 1, 0).reshape(bsz, seq_len, D_HY)


def moe(h2, logits, w1p, b1g, b1l, w2, b2):
    n_tok, d = h2.shape
    i32 = jnp.int32
    top_val, top_idx = lax.top_k(logits, TOP_K)
    weights = jax.nn.softmax(top_val, axis=-1)
    flat_e = top_idx.reshape(-1).astype(i32)
    n_assign = flat_e.shape[0]
    n_blocks = -(-n_assign // MOE_ROWS) + N_EXPERTS
    iota = jnp.arange(n_assign, dtype=i32)
    experts = jnp.arange(N_EXPERTS, dtype=i32)
    e_sorted, order = lax.sort((flat_e, iota), num_keys=1, is_stable=True)
    counts = jnp.sum(flat_e[:, None] == experts[None, :], axis=0).astype(i32)
    padded = (counts + MOE_ROWS - 1) // MOE_ROWS * MOE_ROWS
    end_pad = jnp.cumsum(padded)
    start_pad = end_pad - padded
    start = jnp.cumsum(counts) - counts
    block_start = jnp.arange(n_blocks, dtype=i32) * MOE_ROWS
    block_e = jnp.minimum(jnp.sum(block_start[:, None] >= end_pad[None, :], axis=1), N_EXPERTS - 1).astype(i32)
    off = (block_start - start_pad[block_e])[:, None] + jnp.arange(MOE_ROWS, dtype=i32)[None, :]
    src = jnp.where(off < counts[block_e][:, None], start[block_e][:, None] + off, 0).reshape(-1)
    slot_tok = order[src] // TOP_K
    shift = jnp.sum(jnp.where(e_sorted[:, None] == experts[None, :], (start_pad - start)[None, :], 0), axis=1)
    _, pos = lax.sort((order, iota + shift), num_keys=1)
    n_used = (end_pad[-1] // MOE_ROWS).astype(i32)
    part = n_blocks // MOE_SPLIT
    outs = []
    for s in range(MOE_SPLIT):
        rows = slice(s * part * MOE_ROWS, (s + 1) * part * MOE_ROWS)
        used = jnp.clip(n_used - s * part, 0, part).reshape(1)
        outs.append(moe_expert_blocks(block_e[s * part:(s + 1) * part], used, h2[slot_tok[rows]],
                                      w1p, b1g, b1l, w2, b2))
    out = outs[0] if MOE_SPLIT == 1 else jnp.concatenate(outs, axis=0)
    return out[pos.reshape(n_tok, TOP_K).T.reshape(-1)], weights


def _combine_kernel(*refs, final):
    p_refs = refs[:TOP_K]
    w_ref, x_ref, g2_ref = refs[TOP_K:TOP_K + 3]
    rest = refs[TOP_K + 3:]
    o_ref = rest[-1]
    w = w_ref[...]
    f = w[:, 0:1] * p_refs[0][...].astype(jnp.float32)
    for k in range(1, TOP_K):
        f = f + w[:, k:k + 1] * p_refs[k][...].astype(jnp.float32)
    xn = x_ref[...] + g2_ref[0] * f
    if final:
        ms = jnp.mean(xn * xn, axis=-1, keepdims=True)
        xn = xn * lax.rsqrt(ms + NORM_EPS) * rest[0][...]
    o_ref[...] = xn


def combine_residual(picked, weights, x2d, g2, first_row, rows_per_mod, final_gain=None):
    rows, d = x2d.shape
    n_tok = weights.shape[0]
    tm = MERGE_TM
    blocks_per_mod = rows_per_mod // tm
    first = first_row // tm
    k_blocks = n_tok // tm
    in_specs = [pl.BlockSpec((tm, d), lambda i, k=k: (k * k_blocks + first + i, 0)) for k in range(TOP_K)]
    in_specs += [pl.BlockSpec((tm, TOP_K), lambda i: (first + i, 0)),
                 pl.BlockSpec((tm, d), lambda i: (i, 0)),
                 pl.BlockSpec((1, 1, d), lambda i: (i // blocks_per_mod, 0, 0))]
    args = [picked] * TOP_K + [weights, x2d, g2]
    if final_gain is not None:
        in_specs.append(pl.BlockSpec((1, d), lambda i: (0, 0)))
        args.append(final_gain)
    return pl.pallas_call(
        functools.partial(_combine_kernel, final=final_gain is not None),
        grid=(rows // tm,),
        in_specs=in_specs,
        out_specs=pl.BlockSpec((tm, d), lambda i: (i, 0)),
        out_shape=jax.ShapeDtypeStruct((rows, d), jnp.float32),
        compiler_params=pltpu.CompilerParams(
            dimension_semantics=("parallel",),
            vmem_limit_bytes=VMEM_LIMIT_BYTES),
        name="combine_residual",
    )(*args)


def kernel(x, c, ctx, c_ctx, ada_w, ada_b, norm1_g, norm2_g, final_norm_g, w_in,
           hy_conv_w, hy_conv_b, hy_f_w1, hy_f_b1, hy_f_w2, hy_f_b2, hy_f_w3, hy_f_b3, hy_sin_freq, hy_bias,
           s5_lam_re, s5_lam_im, s5_log_dt, s5_b_re, s5_b_im, s5_c_re, s5_c_im, s5_d, s5_glu_w,
           ml_conv_w, ml_conv_b, ml_gate_b, ml_norm_g,
           w_br_hy, w_br_s5, w_br_ml, w_out,
           moe_router_w, moe_router_b, moe_w1, moe_b1, moe_w2, moe_b2):
    bsz, seq_len, d = x.shape
    ctx_len = ctx.shape[1]
    bf, f32 = jnp.bfloat16, jnp.float32
    silu_c = jax.nn.silu(c.astype(f32))
    silu_cc = jax.nn.silu(c_ctx.astype(f32))[None]
    x2 = x.reshape(bsz * seq_len, d)
    c2 = ctx.reshape(bsz * ctx_len, d)
    for l in range(DEPTH):
        need_ctx = l < DEPTH - 1
        mod_l = (silu_c @ ada_w[l] + ada_b[l])[:, None, :]
        mod_c = (silu_cc @ ada_w[l] + ada_b[l])[:, None, :]
        sh1_l, sc1_l, g1_l, sh2_l, sc2_l, g2_l = jnp.split(mod_l, 6, axis=-1)
        sh1_c, sc1_c, g1_c, sh2_c, sc2_c, g2_c = jnp.split(mod_c, 6, axis=-1)

        w_in_p = permute_w_in(w_in[l])
        gain1 = norm1_g[l].reshape(1, d)
        s5_l, qk_l, v_l, hy_l, o_l, mg_l, gt_l = norm_mod_project(x2, sh1_l, sc1_l, gain1, w_in_p, seq_len)
        s5_c, qk_c, v_c, hy_c, o_c, mg_c, gt_c = norm_mod_project(c2, sh1_c, sc1_c, gain1, w_in_p, bsz * ctx_len)
        per_b = lambda a, n: a.reshape(bsz, n, a.shape[-1])

        hy_params = (hy_f_w1[l], hy_f_b1[l], hy_f_w2[l], hy_f_b2[l], hy_f_w3[l], hy_f_b3[l], hy_sin_freq[l])
        conv_wb = jnp.concatenate([hy_conv_w[l], hy_conv_b[l][None]], axis=0)
        hy_out_l = hyena_mixer(per_b(hy_l, seq_len), conv_wb, hy_bias[l], hyena_reversed_taps(seq_len, hy_params),
                               ch=HY_CH, row_len=GRID_W, pad_batch=bsz)
        s5_w = s5_chunk_weights(s5_lam_re[l], s5_lam_im[l], s5_log_dt[l], s5_b_re[l], s5_b_im[l],
                                s5_c_re[l], s5_c_im[l], s5_d[l])
        s5_out_l, s5_out_c = s5_scan(per_b(s5_l, seq_len), per_b(s5_c, ctx_len), s5_w)
        ml_cw = jnp.concatenate([pad_heads(ml_conv_w[l][:, :D_ML]), pad_heads(ml_conv_w[l][:, D_ML:])], axis=1)
        ml_cb = jnp.concatenate([pad_heads(ml_conv_b[l][:D_ML]), pad_heads(ml_conv_b[l][D_ML:])])[None]
        ml_gb = jnp.pad(ml_gate_b[l], (0, GATE_PAD - 4 * ML_HEADS))[None]
        ml_out_c, ml_out_l = mlstm_call(
            per_b(qk_c, ctx_len), per_b(qk_l, seq_len), per_b(v_c, ctx_len), per_b(v_l, seq_len),
            per_b(gt_c, ctx_len), per_b(gt_l, seq_len), per_b(o_c, ctx_len), per_b(o_l, seq_len),
            ml_cw, ml_cb, ml_gb, pad_heads(ml_norm_g[l])[None], row_len=GRID_W)

        wglu, wh, ws, wo = s5_glu_w[l].astype(bf), w_br_hy[l].astype(bf), w_br_s5[l].astype(bf), w_out[l].astype(bf)
        wm = pad_heads(w_br_ml[l].T).T.astype(bf)
        rw = jnp.pad(moe_router_w[l], ((0, 0), (0, ROUTER_PAD - N_EXPERTS))).astype(bf)
        rb = jnp.pad(moe_router_b[l], (0, ROUTER_PAD - N_EXPERTS))[None]
        n2 = norm2_g[l].reshape(1, d)
        flat = lambda a: a.reshape(-1, a.shape[-1])
        x2, h2_l, lg_l = merge_project_residual(
            flat(hy_out_l), flat(s5_out_l), flat(ml_out_l), mg_l, x2, g1_l, sh2_l, sc2_l, n2,
            wglu, wh, ws, wm, wo, rw, rb, seq_len)

        w1p = deinterleave_cast(moe_w1.reshape(DEPTH * N_EXPERTS * d, 2 * D_FF_EXPERT), l)
        moe_w = (w1p.reshape(N_EXPERTS, d, 2 * D_FF_EXPERT),
                 moe_b1[l][:, None, 0::2], moe_b1[l][:, None, 1::2],
                 moe_w2[l].astype(bf), moe_b2[l][:, None, :])
        if need_ctx:
            hy_out_c = hyena_mixer(per_b(hy_c, ctx_len), conv_wb, hy_bias[l],
                                   hyena_reversed_taps(ctx_len, hy_params),
                                   ch=ctx_len, row_len=ctx_len, pad_batch=LANES)
            c2, h2_c, lg_c = merge_project_residual(
                flat(hy_out_c), flat(s5_out_c), flat(ml_out_c), mg_c, c2, g1_c, sh2_c, sc2_c, n2,
                wglu, wh, ws, wm, wo, rw, rb, bsz * ctx_len)
            tok = jnp.concatenate([h2_c, h2_l], axis=0)
            lg = jnp.concatenate([lg_c, lg_l], axis=0)
            picked, wts = moe(tok, lg[:, :N_EXPERTS], *moe_w)
            c2 = combine_residual(picked, wts, c2, g2_c, 0, bsz * ctx_len)
            x2 = combine_residual(picked, wts, x2, g2_l, bsz * ctx_len, seq_len)
        else:
            picked, wts = moe(h2_l, lg_l[:, :N_EXPERTS], *moe_w)
            x2 = combine_residual(picked, wts, x2, g2_l, 0, seq_len, final_gain=final_norm_g.reshape(1, d))
    return x2.reshape(bsz, seq_len, d)
```

```python
import functools
import math

import jax
import jax.numpy as jnp
from jax import lax
from jax.experimental import pallas as pl
from jax.experimental.pallas import tpu as pltpu

D_MODEL = 1024
DEPTH = 4
GRID_W = 64

D_HY = 384
D_S5 = 384
D_ML = 384
N_BRANCH = 3
SHORT_CONV = 3

HY_ORDER = 2
HY_EMB = 33
HY_BANDS = (HY_EMB - 1) // 2
HY_DECAY_TARGET = 1e-2
HY_FAST_DECAY_PCT = 0.3
HY_SLOW_DECAY_PCT = 1.5
HY_MIN_DECAY = math.log(HY_DECAY_TARGET) / HY_SLOW_DECAY_PCT
HY_MAX_DECAY = math.log(HY_DECAY_TARGET) / HY_FAST_DECAY_PCT

S5_GROUP = 16
S5_GROUPS = D_S5 // S5_GROUP
S5_STATE = 64

ML_HEADS = 4
ML_HEAD_DIM = D_ML // ML_HEADS
NEG = -1e30

N_EXPERTS = 32
TOP_K = 4
D_FF_EXPERT = 512
SWIGLU_LIMIT = 7.0
SWIGLU_ALPHA = 1.702

NORM_EPS = 1e-6

COL_SIZES = (D_S5, 2 * D_ML, D_ML, 4 * ML_HEADS, (1 + HY_ORDER) * D_HY, D_ML, N_BRANCH * D_MODEL)

LANES = 128
VMEM_LIMIT_BYTES = 56 * 1024 * 1024

ML_HP = LANES
ML_DP = ML_HEADS * ML_HP
ML_TC = 256
GATE_PAD = LANES

PROJ_OUT = (("s5", D_S5, jnp.bfloat16), ("qk", 2 * ML_DP, jnp.bfloat16), ("v", ML_DP, jnp.bfloat16),
            ("hy", (1 + HY_ORDER) * D_HY, jnp.bfloat16), ("o", ML_DP, jnp.bfloat16),
            ("mg", N_BRANCH * D_MODEL, jnp.bfloat16), ("gt", GATE_PAD, jnp.float32))
PROJ_COLS = sum(w for _, w, _ in PROJ_OUT)
PROJ_TM = 512

MOE_ROWS = 512
MOE_SPLIT = 1
MERGE_TM = 512
ROUTER_PAD = LANES

HY_CB = 16
HI16 = -65536
HY_CH = 128


def _proj_kernel(x_ref, shift_ref, scale_ref, g_ref, w_ref, *o_refs):
    x = x_ref[...]
    ms = jnp.mean(x * x, axis=-1, keepdims=True)
    y = x * lax.rsqrt(ms + NORM_EPS) * g_ref[...]
    h = (y * (1.0 + scale_ref[0]) + shift_ref[0]).astype(jnp.bfloat16)
    start = 0
    for o_ref, (_, width, _) in zip(o_refs, PROJ_OUT):
        o_ref[...] = jnp.dot(h, w_ref[:, start:start + width],
                             preferred_element_type=jnp.float32).astype(o_ref.dtype)
        start += width


def norm_mod_project(x2d, shift, scale, gain, w_bf16, rows_per_mod):
    rows, d = x2d.shape
    tm = PROJ_TM
    blocks_per_mod = rows_per_mod // tm
    return pl.pallas_call(
        _proj_kernel,
        grid=(rows // tm,),
        in_specs=[
            pl.BlockSpec((tm, d), lambda i: (i, 0)),
            pl.BlockSpec((1, 1, d), lambda i: (i // blocks_per_mod, 0, 0)),
            pl.BlockSpec((1, 1, d), lambda i: (i // blocks_per_mod, 0, 0)),
            pl.BlockSpec((1, d), lambda i: (0, 0)),
            pl.BlockSpec((d, PROJ_COLS), lambda i: (0, 0)),
        ],
        out_specs=[pl.BlockSpec((tm, w), lambda i: (i, 0)) for _, w, _ in PROJ_OUT],
        out_shape=[jax.ShapeDtypeStruct((rows, w), dt) for _, w, dt in PROJ_OUT],
        compiler_params=pltpu.CompilerParams(
            dimension_semantics=("parallel",),
            vmem_limit_bytes=VMEM_LIMIT_BYTES),
        name="norm_mod_project",
    )(x2d, shift, scale, gain, w_bf16)


def pad_heads(w):
    lead = w.shape[:-1]
    w = w.reshape(lead + (ML_HEADS, ML_HEAD_DIM))
    w = jnp.pad(w, [(0, 0)] * len(lead) + [(0, 0), (0, ML_HP - ML_HEAD_DIM)])
    return w.reshape(lead + (ML_DP,))


def permute_w_in(w_in):
    cols, start = [], 0
    for s in COL_SIZES:
        cols.append(w_in[:, start:start + s])
        start += s
    s5, qk, v, gt, hy, o, mg = cols
    qk = jnp.concatenate([pad_heads(qk[:, :D_ML]), pad_heads(qk[:, D_ML:])], axis=1)
    gt = jnp.pad(gt, ((0, 0), (0, GATE_PAD - gt.shape[1])))
    return jnp.concatenate([s5, qk, pad_heads(v), hy, pad_heads(o), mg, gt], axis=1).astype(jnp.bfloat16)


def _merge_kernel(hy_ref, s5_ref, ml_ref, mg_ref, x_ref, g1_ref, sh2_ref, sc2_ref, n2_ref,
                  wglu_ref, wh_ref, ws_ref, wm_ref, wo_ref, rw_ref, rb_ref,
                  xo_ref, h2_ref, lg_ref):
    bf = jnp.bfloat16
    f32 = jnp.float32
    d = D_MODEL
    g = jax.nn.gelu(s5_ref[...].astype(f32))
    s5 = g * jax.nn.sigmoid(jnp.dot(g.astype(bf), wglu_ref[...], preferred_element_type=f32))
    y = jax.nn.sigmoid(mg_ref[:, 0:d].astype(f32)) * jnp.dot(hy_ref[...], wh_ref[...], preferred_element_type=f32)
    y = y + jax.nn.sigmoid(mg_ref[:, d:2 * d].astype(f32)) * jnp.dot(s5.astype(bf), ws_ref[...],
                                                                     preferred_element_type=f32)
    y = y + jax.nn.sigmoid(mg_ref[:, 2 * d:3 * d].astype(f32)) * jnp.dot(ml_ref[...], wm_ref[...],
                                                                         preferred_element_type=f32)
    out = jnp.dot(y.astype(bf), wo_ref[...], preferred_element_type=f32)
    xn = x_ref[...] + g1_ref[0] * out
    xo_ref[...] = xn
    ms = jnp.mean(xn * xn, axis=-1, keepdims=True)
    h2 = (xn * lax.rsqrt(ms + NORM_EPS) * n2_ref[...]) * (1.0 + sc2_ref[0]) + sh2_ref[0]
    h2b = h2.astype(bf)
    h2_ref[...] = h2b
    lg_ref[...] = jnp.dot(h2b, rw_ref[...], preferred_element_type=f32) + rb_ref[...]


def merge_project_residual(hy, s5, ml, mg, x2d, g1, sh2, sc2, n2, wglu, wh, ws, wm, wo, rw, rb, rows_per_mod):
    rows, d = x2d.shape
    tm = MERGE_TM
    blocks_per_mod = rows_per_mod // tm
    row_spec = lambda c: pl.BlockSpec((tm, c), lambda i: (i, 0))
    mod_spec = pl.BlockSpec((1, 1, d), lambda i: (i // blocks_per_mod, 0, 0))
    full = lambda a: pl.BlockSpec(a.shape, lambda i: (0, 0))
    return pl.pallas_call(
        _merge_kernel,
        grid=(rows // tm,),
        in_specs=[row_spec(D_HY), row_spec(D_S5), row_spec(ML_DP), row_spec(N_BRANCH * d), row_spec(d),
                  mod_spec, mod_spec, mod_spec, full(n2),
                  full(wglu), full(wh), full(ws), full(wm), full(wo), full(rw), full(rb)],
        out_specs=[row_spec(d), row_spec(d), row_spec(ROUTER_PAD)],
        out_shape=[jax.ShapeDtypeStruct((rows, d), jnp.float32),
                   jax.ShapeDtypeStruct((rows, d), jnp.bfloat16),
                   jax.ShapeDtypeStruct((rows, ROUTER_PAD), jnp.float32)],
        compiler_params=pltpu.CompilerParams(
            dimension_semantics=("parallel",),
            vmem_limit_bytes=VMEM_LIMIT_BYTES),
        name="merge_project_residual",
    )(hy, s5, ml, mg, x2d, g1, sh2, sc2, n2, wglu, wh, ws, wm, wo, rw, rb)


def _deinterleave_kernel(w_ref, p_ref, o_ref):
    o_ref[...] = jnp.dot(w_ref[...].astype(jnp.bfloat16), p_ref[...],
                         preferred_element_type=jnp.float32).astype(jnp.bfloat16)


def deinterleave_cast(w_all, layer):
    rows, n = w_all.shape[0] // DEPTH, w_all.shape[1]
    tm = 1024
    first = layer * (rows // tm)
    src = jnp.concatenate([jnp.arange(0, n, 2), jnp.arange(1, n, 2)])
    perm = (jnp.arange(n)[:, None] == src[None, :]).astype(jnp.bfloat16)
    return pl.pallas_call(
        _deinterleave_kernel,
        grid=(rows // tm,),
        in_specs=[pl.BlockSpec((tm, n), lambda i: (first + i, 0)), pl.BlockSpec((n, n), lambda i: (0, 0))],
        out_specs=pl.BlockSpec((tm, n), lambda i: (i, 0)),
        out_shape=jax.ShapeDtypeStruct((rows, n), jnp.bfloat16),
        compiler_params=pltpu.CompilerParams(
            dimension_semantics=("parallel",),
            vmem_limit_bytes=VMEM_LIMIT_BYTES),
        name="deinterleave_cast",
    )(w_all, perm)


def _moe_kernel(be_ref, nu_ref, x_ref, w1_ref, b1g_ref, b1l_ref, w2_ref, b2_ref, o_ref):
    f = D_FF_EXPERT

    @pl.when(pl.program_id(0) < nu_ref[0])
    def _():
        x = x_ref[...]
        hg = jnp.dot(x, w1_ref[0, :, 0:f], preferred_element_type=jnp.float32) + b1g_ref[0]
        hl = jnp.dot(x, w1_ref[0, :, f:2 * f], preferred_element_type=jnp.float32) + b1l_ref[0]
        x_glu = jnp.minimum(hg, SWIGLU_LIMIT)
        x_lin = jnp.clip(hl, -SWIGLU_LIMIT, SWIGLU_LIMIT)
        act = x_glu * jax.nn.sigmoid(SWIGLU_ALPHA * x_glu) * (x_lin + 1.0)
        out = jnp.dot(act.astype(jnp.bfloat16), w2_ref[0], preferred_element_type=jnp.float32) + b2_ref[0]
        o_ref[...] = out.astype(o_ref.dtype)


def moe_expert_blocks(block_e, n_used, x_sorted, w1p, b1g, b1l, w2, b2):
    n_slots, d = x_sorted.shape
    n_blocks = n_slots // MOE_ROWS
    f = D_FF_EXPERT

    def row_map(i, be, nu):
        return (jnp.maximum(jnp.minimum(i, nu[0] - 1), 0), 0)

    def w_map(i, be, nu):
        return (be[jnp.maximum(jnp.minimum(i, nu[0] - 1), 0)], 0, 0)

    grid_spec = pltpu.PrefetchScalarGridSpec(
        num_scalar_prefetch=2,
        grid=(n_blocks,),
        in_specs=[
            pl.BlockSpec((MOE_ROWS, d), row_map),
            pl.BlockSpec((1, d, 2 * f), w_map),
            pl.BlockSpec((1, 1, f), w_map),
            pl.BlockSpec((1, 1, f), w_map),
            pl.BlockSpec((1, f, d), w_map),
            pl.BlockSpec((1, 1, d), w_map),
        ],
        out_specs=pl.BlockSpec((MOE_ROWS, d), row_map),
    )
    return pl.pallas_call(
        _moe_kernel,
        grid_spec=grid_spec,
        out_shape=jax.ShapeDtypeStruct((n_slots, d), jnp.bfloat16),
        compiler_params=pltpu.CompilerParams(
            dimension_semantics=("arbitrary",),
            vmem_limit_bytes=VMEM_LIMIT_BYTES),
        name="moe_expert_blocks",
    )(block_e, n_used, x_sorted, w1p, b1g, b1l, w2, b2)


S5_T = 16
S5_W = S5_T * S5_GROUP
S5_HALF = 2 * S5_STATE


def _s5_kernel(u_ref, wp_ref, wm_ref, wq0_ref, wq1_ref, ar_ref, ai_ref, y_ref, s_scr, x_scr, yb_scr, ur_scr,
               *, n_batch, ctx_chunks, lat_chunks):
    nb = n_batch
    u = u_ref[0]
    segments = ((0, ctx_chunks), (ctx_chunks * nb, lat_chunks))
    for seg_start, seg_chunks in segments:
        for i in range(seg_chunks):
            dst = seg_start + nb * i
            src = seg_start + nb * (seg_chunks - 1 - i)
            ur_scr[dst:dst + nb, :] = u_ref[0, src:src + nb, :]
    s_scr[...] = (jnp.dot(u, wp_ref[0, 0:S5_W, :], preferred_element_type=jnp.float32)
                  + jnp.dot(ur_scr[...], wp_ref[0, S5_W:2 * S5_W, :], preferred_element_type=jnp.float32))
    ar = jnp.broadcast_to(ar_ref[0], (nb, S5_HALF))
    ai = jnp.broadcast_to(ai_ref[0], (nb, S5_HALF))

    def step(i, carry):
        xr, xi = carry
        r = pl.multiple_of(i * nb, nb)
        x_scr[pl.ds(r, nb), 0:S5_HALF] = xr
        x_scr[pl.ds(r, nb), S5_HALF:2 * S5_HALF] = xi
        sr = s_scr[pl.ds(r, nb), 0:S5_HALF]
        si = s_scr[pl.ds(r, nb), S5_HALF:2 * S5_HALF]
        return ar * xr - ai * xi + sr, ar * xi + ai * xr + si

    zero = jnp.zeros((nb, S5_HALF), jnp.float32)
    lax.fori_loop(0, ctx_chunks + lat_chunks, step, (zero, zero))

    xin = x_scr[...].astype(jnp.bfloat16)
    s_scr[...] = (jnp.dot(u, wm_ref[0], preferred_element_type=jnp.float32)
                  + jnp.dot(xin, wq0_ref[0], preferred_element_type=jnp.float32))
    yb_scr[...] = jnp.dot(xin, wq1_ref[0], preferred_element_type=jnp.float32)
    for seg_start, seg_chunks in segments:
        for i in range(seg_chunks):
            dst = seg_start + nb * i
            src = seg_start + nb * (seg_chunks - 1 - i)
            y_ref[0, dst:dst + nb, :] = (s_scr[dst:dst + nb, :] + yb_scr[src:src + nb, :]).astype(y_ref.dtype)


def s5_scan_call(u, wp, wm, wq0, wq1, ar, ai, *, n_batch, ctx_chunks):
    groups, rows, _ = u.shape
    lat_chunks = rows // n_batch - ctx_chunks
    per_group = lambda a: pl.BlockSpec((1,) + a.shape[1:], lambda g: (g, 0, 0))
    return pl.pallas_call(
        functools.partial(_s5_kernel, n_batch=n_batch, ctx_chunks=ctx_chunks, lat_chunks=lat_chunks),
        grid=(groups,),
        in_specs=[per_group(a) for a in (u, wp, wm, wq0, wq1, ar, ai)],
        out_specs=pl.BlockSpec((1, rows, S5_W), lambda g: (g, 0, 0)),
        out_shape=jax.ShapeDtypeStruct((groups, rows, S5_W), jnp.bfloat16),
        scratch_shapes=[pltpu.VMEM((rows, S5_W), jnp.float32)] * 3 + [pltpu.VMEM((rows, S5_W), jnp.bfloat16)],
        compiler_params=pltpu.CompilerParams(
            dimension_semantics=("parallel",),
            vmem_limit_bytes=VMEM_LIMIT_BYTES),
        name="s5_scan",
    )(u, wp, wm, wq0, wq1, ar, ai)


def s5_chunk_weights(lam_re, lam_im, log_dt, b_re, b_im, c_re, c_im, d_skip):
    f32 = jnp.float32
    t_len, g_n, p_n, n_n = S5_T, S5_GROUPS, S5_STATE, S5_GROUP
    b_mat = lax.complex(b_re.astype(f32), b_im.astype(f32))
    c_mat = lax.complex(c_re.astype(f32), c_im.astype(f32))
    lam = lax.complex(lam_re.astype(f32), lam_im.astype(f32))
    lam_dt = lam * jnp.exp(log_dt.astype(f32))[..., None]
    b_bar = ((jnp.exp(lam_dt) - 1.0) / lam)[..., None] * b_mat
    j = jnp.arange(t_len + 1, dtype=f32)
    pw = jnp.exp(j[:, None, None, None] * lam_dt[None])

    kern = [jnp.real(jnp.einsum('gnp,jgp,gpm->jgnm', c_mat, pw[:t_len, d], b_bar[d])) for d in range(2)]
    idx = jnp.arange(t_len)
    diff = idx[None, :] - idx[:, None]
    k0 = kern[0][jnp.clip(diff, 0, None)]
    k1 = kern[1][jnp.clip(-diff, 0, None)]
    m5 = (jnp.where((diff >= 0)[:, :, None, None, None], k0, 0.0)
          + jnp.where((diff <= 0)[:, :, None, None, None], k1, 0.0))
    skip = jnp.eye(t_len, dtype=f32)[:, :, None, None, None] * (
        d_skip.astype(f32).reshape(g_n, n_n)[None, None, :, :, None] * jnp.eye(n_n, dtype=f32)[None, None, None])
    wm = (m5 + skip).transpose(2, 0, 4, 1, 3).reshape(g_n, S5_W, S5_W)

    pf = jnp.einsum('sgp,gpm->gsmp', pw[t_len - 1 - idx, 0], b_bar[0]).reshape(g_n, S5_W, p_n)
    pb = jnp.einsum('sgp,gpm->gsmp', pw[idx, 1], b_bar[1]).reshape(g_n, S5_W, p_n)
    z = jnp.zeros_like(jnp.real(pf))
    wp = jnp.concatenate([
        jnp.concatenate([jnp.real(pf), z, jnp.imag(pf), z], axis=-1),
        jnp.concatenate([z, jnp.real(pb), z, jnp.imag(pb)], axis=-1)], axis=1)

    q0 = jnp.einsum('gnp,tgp->gptn', c_mat, pw[idx + 1, 0]).reshape(g_n, p_n, S5_W)
    q1 = jnp.einsum('gnp,tgp->gptn', c_mat, pw[t_len - idx, 1]).reshape(g_n, p_n, S5_W)
    zq = jnp.zeros_like(jnp.real(q0))
    wq0 = jnp.concatenate([jnp.real(q0), zq, -jnp.imag(q0), zq], axis=1)
    wq1 = jnp.concatenate([zq, jnp.real(q1), zq, -jnp.imag(q1)], axis=1)
    lam_t = pw[t_len]
    ar = jnp.concatenate([jnp.real(lam_t[0]), jnp.real(lam_t[1])], axis=-1)[:, None, :]
    ai = jnp.concatenate([jnp.imag(lam_t[0]), jnp.imag(lam_t[1])], axis=-1)[:, None, :]
    bf = jnp.bfloat16
    return wp.astype(bf), wm.astype(bf), wq0.astype(bf), wq1.astype(bf), ar, ai


def s5_scan(u_l, u_c, weights):
    bsz, seq_len, _ = u_l.shape
    ctx_len = u_c.shape[1]

    def to_chunks(a):
        a = a.reshape(bsz, a.shape[1] // S5_T, S5_T, S5_GROUPS, S5_GROUP).transpose(3, 1, 0, 2, 4)
        return a.reshape(S5_GROUPS, -1, S5_W)

    u = jnp.concatenate([to_chunks(u_c), to_chunks(u_l)], axis=1).astype(jnp.bfloat16)
    y = s5_scan_call(u, *weights, n_batch=bsz, ctx_chunks=ctx_len // S5_T)

    def from_chunks(a, length):
        a = a.reshape(S5_GROUPS, length // S5_T, bsz, S5_T, S5_GROUP)
        return a.transpose(2, 1, 3, 0, 4).reshape(bsz, length, D_S5)

    ctx_rows = ctx_len // S5_T * bsz
    return from_chunks(y[:, ctx_rows:], seq_len), from_chunks(y[:, :ctx_rows], ctx_len)


def _dot01(a01, x):
    bf, f32 = jnp.bfloat16, jnp.float32
    hi = x.astype(bf)
    r1 = x - hi.astype(f32)
    mid = r1.astype(bf)
    lo = (r1 - mid.astype(f32)).astype(bf)
    d = lambda y: jnp.dot(a01, y, preferred_element_type=f32)
    return d(hi) + d(mid) + d(lo)


def _mlstm_kernel(qkc_ref, qkl_ref, vc_ref, vl_ref, gc_ref, gl_ref, oc_ref, ol_ref,
                  cw_ref, cb_ref, gb_ref, ng_ref,
                  outc_ref, outl_ref,
                  q_scr, k_scr, v_scr, g_scr, h_scr, c_scr, n_scr, m_scr,
                  *, ctx_len, seq_len, row_len):
    bf, f32 = jnp.bfloat16, jnp.float32
    tc = ML_TC
    nc_ctx = ctx_len // tc
    nc_lat = seq_len // tc
    nc_tot = nc_ctx + nc_lat

    def conv_silu(x, n_rows, rlen):
        row = lax.broadcasted_iota(jnp.int32, (n_rows, 1), 0) % rlen
        prev = jnp.where(row == 0, 0.0, pltpu.roll(x, 1, 0))
        nxt = jnp.where(row == rlen - 1, 0.0, pltpu.roll(x, n_rows - 1, 0))
        y = cb_ref[...] + prev * cw_ref[0:1, :] + x * cw_ref[1:2, :] + nxt * cw_ref[2:3, :]
        return y * jax.nn.sigmoid(y)

    kscale = ML_HEAD_DIM ** -0.5
    a = conv_silu(qkc_ref[0].astype(f32), ctx_len, ctx_len)
    q_scr[0:ctx_len, :] = a[:, 0:ML_DP].astype(bf)
    k_scr[0:ctx_len, :] = (a[:, ML_DP:2 * ML_DP] * kscale).astype(bf)
    v_scr[0:ctx_len, :] = vc_ref[0]
    g_scr[0:ctx_len, :] = gc_ref[0] + gb_ref[...]

    def prep(c, _):
        r = pl.multiple_of(c * tc, tc)
        a = conv_silu(qkl_ref[0, pl.ds(r, tc), :].astype(f32), tc, row_len)
        q_scr[pl.ds(ctx_len + r, tc), :] = a[:, 0:ML_DP].astype(bf)
        k_scr[pl.ds(ctx_len + r, tc), :] = (a[:, ML_DP:2 * ML_DP] * kscale).astype(bf)
        v_scr[pl.ds(ctx_len + r, tc), :] = vl_ref[0, pl.ds(r, tc), :]
        g_scr[pl.ds(ctx_len + r, tc), :] = gl_ref[0, pl.ds(r, tc), :] + gb_ref[...]
        return 0

    lax.fori_loop(0, nc_lat, prep, 0)

    ti = lax.broadcasted_iota(jnp.int32, (tc, tc), 0)
    si = lax.broadcasted_iota(jnp.int32, (tc, tc), 1)

    for d in range(2):
        causal = (si <= ti) if d == 0 else (si >= ti)
        tri = causal.astype(bf)
        c_scr[...] = jnp.zeros_like(c_scr)
        n_scr[...] = jnp.zeros_like(n_scr)
        m_scr[...] = jnp.full_like(m_scr, NEG)

        def chunk(i, _, d=d, causal=causal, tri=tri):
            if d == 0:
                ci = i
            else:
                ci = jnp.where(i < nc_ctx, nc_ctx - 1 - i, nc_tot + nc_ctx - 1 - i)
            r0 = pl.multiple_of(ci * tc, tc)
            gts = g_scr[pl.ds(r0, tc), :]
            logf = jax.nn.log_sigmoid(gts)
            bm = _dot01(tri, logf)
            gts_t = gts.T
            bm_t = bm.T
            b_end_row = bm[tc - 1:tc, :] if d == 0 else bm[0:1, :]
            for h in range(ML_HEADS):
                icol = h + 8 * d
                fcol = ML_HEADS + h + 8 * d
                lanes = slice(h * ML_HP, (h + 1) * ML_HP)
                i_col = gts[:, icol:icol + 1]
                b_col = bm[:, fcol:fcol + 1]
                i_row = gts_t[icol:icol + 1, :]
                b_row = bm_t[fcol:fcol + 1, :]
                b_end = b_end_row[:, fcol:fcol + 1]
                m_prev = m_scr[h:h + 1, 0:1]
                dmat = jnp.where(causal, b_col - b_row + i_row, NEG)
                a_col = b_col + m_prev
                m_out = jnp.maximum(a_col, jnp.max(dmat, axis=1, keepdims=True))
                wmat = jnp.exp(dmat - m_out)
                wa = jnp.exp(a_col - m_out)
                q = q_scr[pl.ds(r0, tc), lanes]
                k = k_scr[pl.ds(r0, tc), lanes]
                v = v_scr[pl.ds(r0, tc), lanes]
                s = lax.dot_general(q, k, (((1,), (1,)), ((), ())), preferred_element_type=f32) * wmat
                c_mem = c_scr[h * ML_HP:(h + 1) * ML_HP, :]
                n_row = n_scr[h:h + 1, :]
                qf = q.astype(f32)
                num = (jnp.dot(s.astype(bf), v, preferred_element_type=f32)
                       + wa * lax.dot_general(q, c_mem.astype(bf), (((1,), (1,)), ((), ())),
                                              preferred_element_type=f32))
                den = jnp.sum(s, axis=1, keepdims=True) + wa * jnp.sum(qf * n_row, axis=1, keepdims=True)
                hval = num / jnp.maximum(jnp.abs(den), jnp.exp(-m_out))
                if d == 0:
                    h_scr[pl.ds(r0, tc), lanes] = hval
                else:
                    h_scr[pl.ds(r0, tc), lanes] += hval
                g_col = b_end - b_col + i_col
                m_new = jnp.maximum(b_end + m_prev, jnp.max(g_col, axis=0, keepdims=True))
                decay = jnp.exp(b_end + m_prev - m_new)
                w_col = jnp.exp(g_col - m_new)
                vw = (v.astype(f32) * w_col).astype(bf)
                c_scr[h * ML_HP:(h + 1) * ML_HP, :] = decay * c_mem + lax.dot_general(
                    vw, k, (((0,), (0,)), ((), ())), preferred_element_type=f32)
                n_scr[h:h + 1, :] = decay * n_row + jnp.sum(k.astype(f32) * w_col, axis=0, keepdims=True)
                m_scr[h:h + 1, :] = jnp.broadcast_to(m_new, (1, LANES))
            return 0

        lax.fori_loop(0, nc_tot, chunk, 0)

    def head_out(hs, o):
        outs = []
        for h in range(ML_HEADS):
            x = hs[:, h * ML_HP:(h + 1) * ML_HP]
            ms = jnp.sum(x * x, axis=1, keepdims=True) * (1.0 / ML_HEAD_DIM)
            outs.append(x * lax.rsqrt(ms + NORM_EPS))
        hn = jnp.concatenate(outs, axis=1) * ng_ref[...]
        return (hn * jax.nn.sigmoid(o.astype(f32))).astype(bf)

    outc_ref[0] = head_out(h_scr[0:ctx_len, :], oc_ref[0])

    def fin(c, _):
        r = pl.multiple_of(c * tc, tc)
        outl_ref[0, pl.ds(r, tc), :] = head_out(h_scr[pl.ds(ctx_len + r, tc), :], ol_ref[0, pl.ds(r, tc), :])
        return 0

    lax.fori_loop(0, nc_lat, fin, 0)


def mlstm_call(qk_c, qk_l, v_c, v_l, g_c, g_l, o_c, o_l, conv_w, conv_b, gate_b, norm_g, *, row_len):
    bsz, ctx_len, _ = qk_c.shape
    seq_len = qk_l.shape[1]
    tot = ctx_len + seq_len
    per_b = lambda a: pl.BlockSpec((1,) + a.shape[1:], lambda b: (b, 0, 0))
    full = lambda a: pl.BlockSpec(a.shape, lambda b: (0, 0))
    return pl.pallas_call(
        functools.partial(_mlstm_kernel, ctx_len=ctx_len, seq_len=seq_len, row_len=row_len),
        grid=(bsz,),
        in_specs=[per_b(a) for a in (qk_c, qk_l, v_c, v_l, g_c, g_l, o_c, o_l)]
        + [full(a) for a in (conv_w, conv_b, gate_b, norm_g)],
        out_specs=[pl.BlockSpec((1, ctx_len, ML_DP), lambda b: (b, 0, 0)),
                   pl.BlockSpec((1, seq_len, ML_DP), lambda b: (b, 0, 0))],
        out_shape=[jax.ShapeDtypeStruct((bsz, ctx_len, ML_DP), jnp.bfloat16),
                   jax.ShapeDtypeStruct((bsz, seq_len, ML_DP), jnp.bfloat16)],
        scratch_shapes=[pltpu.VMEM((tot, ML_DP), jnp.bfloat16)] * 3
        + [pltpu.VMEM((tot, GATE_PAD), jnp.float32), pltpu.VMEM((tot, ML_DP), jnp.float32),
           pltpu.VMEM((ML_DP, ML_HP), jnp.float32), pltpu.VMEM((8, LANES), jnp.float32),
           pltpu.VMEM((8, LANES), jnp.float32)],
        compiler_params=pltpu.CompilerParams(
            dimension_semantics=("parallel",),
            vmem_limit_bytes=VMEM_LIMIT_BYTES),
        name="mlstm",
    )(qk_c, qk_l, v_c, v_l, g_c, g_l, o_c, o_l, conv_w, conv_b, gate_b, norm_g)


def _hyena_kernel(cw_ref, hb_ref, v_ref, x1_ref, x2_ref, fr_ref, o_ref, ga_scr, gb_scr, *, ch, na, nb, row_len):
    bf, f32 = jnp.bfloat16, jnp.float32
    nl = na * nb
    cblk = pl.program_id(0)
    row = lax.broadcasted_iota(jnp.int32, (ch, 1), 0) % row_len
    first, last = row == 0, row == row_len - 1
    lane = lax.broadcasted_iota(jnp.int32, (1, nl), 1)
    below = lax.broadcasted_iota(jnp.int32, (ch, ch), 1) < lax.broadcasted_iota(jnp.int32, (ch, ch), 0)

    def short_conv(x_ref, j, part):
        x = x_ref[j].astype(f32)
        cidx = part * D_HY + cblk * HY_CB + j
        prev = jnp.where(first, 0.0, pltpu.roll(x, 1, 0))
        nxt = jnp.where(last, 0.0, pltpu.roll(x, ch - 1, 0))
        return cw_ref[3, cidx] + cw_ref[0, cidx] * prev + cw_ref[1, cidx] * x + cw_ref[2, cidx] * nxt

    def long_conv(g_ref, z):
        zp = jnp.dot(g_ref[...], z.astype(bf), preferred_element_type=f32)
        y = zp[(na - 1) * ch:na * ch]
        for p in range(1, na):
            zpos = zp[(na - 1 - p) * ch:(na - p) * ch]
            zneg = zp[(2 * na - 1 - p) * ch:(2 * na - p) * ch]
            y = y + pltpu.roll(jnp.where(lane < (na - p) * nb, zpos, zneg), p * nb, 1)
        return y

    def channel_pair(jp, _):
        js = (2 * jp, 2 * jp + 1)
        zs = [short_conv(v_ref, j, 0) for j in js]
        for o, xg_ref in ((0, x1_ref), (1, x2_ref)):
            gates = [short_conv(xg_ref, j, 1 + o) for j in js]
            taps = fr_ref[o, pl.ds(jp, 1), :]
            r_lo = pltpu.roll(jnp.broadcast_to(taps[:, 0:ch], (ch, ch)), 0, 1, stride=1, stride_axis=0)
            for q in range(2 * na - 1):
                r_hi = pltpu.roll(jnp.broadcast_to(taps[:, (q + 1) * ch:(q + 2) * ch], (ch, ch)),
                                  0, 1, stride=1, stride_axis=0)
                g = jnp.where(below, r_lo, r_hi)
                ga_scr[q * ch:(q + 1) * ch, :] = lax.bitcast_convert_type(g & HI16, f32).astype(bf)
                gb_scr[q * ch:(q + 1) * ch, :] = lax.bitcast_convert_type(g << 16, f32).astype(bf)
                r_lo = r_hi
            for i, (j, g_ref) in enumerate(zip(js, (ga_scr, gb_scr))):
                y = long_conv(g_ref, zs[i])
                zs[i] = gates[i] * (y + hb_ref[o, cblk * HY_CB + j] * zs[i])
        for i, j in enumerate(js):
            o_ref[j] = zs[i].astype(bf)
        return 0

    lax.fori_loop(0, HY_CB // 2, channel_pair, 0)


def hyena_call(conv_wb, hy_bias, ht, fr, *, ch, na, nb, row_len):
    nl = na * nb
    n_blk = D_HY // HY_CB
    part = lambda k: pl.BlockSpec((HY_CB, ch, nl), lambda i, *_: (i + k * n_blk, 0, 0))
    grid_spec = pltpu.PrefetchScalarGridSpec(
        num_scalar_prefetch=2,
        grid=(n_blk,),
        in_specs=[part(0), part(1), part(2),
                  pl.BlockSpec((HY_ORDER, HY_CB // 2, 2 * ch * na), lambda i, *_: (0, i, 0))],
        out_specs=pl.BlockSpec((HY_CB, ch, nl), lambda i, *_: (i, 0, 0)),
        scratch_shapes=[pltpu.VMEM(((2 * na - 1) * ch, ch), jnp.bfloat16)] * 2,
    )
    return pl.pallas_call(
        functools.partial(_hyena_kernel, ch=ch, na=na, nb=nb, row_len=row_len),
        grid_spec=grid_spec,
        out_shape=jax.ShapeDtypeStruct((D_HY, ch, nl), jnp.bfloat16),
        compiler_params=pltpu.CompilerParams(
            dimension_semantics=("parallel",),
            vmem_limit_bytes=VMEM_LIMIT_BYTES),
        name="hyena",
    )(conv_wb, hy_bias, ht, ht, ht, fr)


def hyena_filters(seq_len, w1, b1, w2, b2, w3, b3, sin_freq):
    t = jnp.linspace(0.0, 1.0, seq_len, dtype=jnp.float32)[:, None]
    w = 2.0 * math.pi * jnp.arange(seq_len, dtype=jnp.float32)[:, None] / seq_len
    f = jnp.linspace(1e-4, HY_BANDS - 1, HY_BANDS, dtype=jnp.float32)[None, :]
    z = jnp.concatenate([t, jnp.cos(f * w), -jnp.sin(f * w)], axis=-1)
    hdn = jnp.sin(sin_freq[0] * (z @ w1 + b1))
    hdn = jnp.sin(sin_freq[1] * (hdn @ w2 + b2))
    hf = (hdn @ w3 + b3).astype(jnp.float32).reshape(seq_len, HY_ORDER, 2, D_HY)
    deltas = jnp.abs(jnp.linspace(HY_MIN_DECAY, HY_MAX_DECAY, D_HY, dtype=jnp.float32))
    decay = jnp.exp(-t * deltas)
    return hf * decay[:, None, None, :]


def hyena_reversed_taps(seq_len, filter_params):
    filt = hyena_filters(seq_len, *filter_params)
    out = []
    for o in range(HY_ORDER):
        h_fwd, h_bwd = filt[:, o, 0], filt[:, o, 1]
        k = jnp.concatenate([h_fwd, jnp.zeros_like(h_fwd[:1]), h_bwd[:0:-1]], axis=0)
        k = k / jnp.sum(jnp.abs(k), axis=0, keepdims=True)
        idx = (seq_len - jnp.arange(2 * seq_len)) % (2 * seq_len)
        out.append(k[idx].T)
    bits = lax.bitcast_convert_type(jnp.stack(out).astype(jnp.bfloat16), jnp.uint16).astype(jnp.uint32)
    bits = bits.reshape(HY_ORDER, D_HY // 2, 2, 2 * seq_len)
    return lax.bitcast_convert_type((bits[:, :, 0] << 16) | bits[:, :, 1], jnp.int32)


def hyena_mixer(hy, conv_wb, hy_bias, fr, *, ch, row_len, pad_batch):
    bsz, seq_len, width = hy.shape
    na = seq_len // ch
    nb = pad_batch
    ht = hy.reshape(bsz, na, ch, width).transpose(3, 2, 1, 0)
    if nb != bsz:
        ht = jnp.pad(ht, ((0, 0), (0, 0), (0, 0), (0, nb - bsz)))
    out = hyena_call(conv_wb, hy_bias, ht.reshape(width, ch, na * nb), fr, ch=ch, na=na, nb=nb, row_len=row_len)
    out = out.reshape(D_HY, ch, na, nb)[..., :bsz]
    return out.transpose(3, 2, 1, 0).reshape(bsz, seq_len, D_HY)


def moe(h2, logits, w1p, b1g, b1l, w2, b2, after_routing=None):
    n_tok, d = h2.shape
    i32 = jnp.int32
    top_val, top_idx = lax.top_k(logits, TOP_K)
    weights = jax.nn.softmax(top_val, axis=-1)
    flat_e = top_idx.reshape(-1).astype(i32)
    n_assign = flat_e.shape[0]
    n_blocks = -(-n_assign // MOE_ROWS) + N_EXPERTS
    iota = jnp.arange(n_assign, dtype=i32)
    experts = jnp.arange(N_EXPERTS, dtype=i32)
    e_sorted, order = lax.sort((flat_e, iota), num_keys=1, is_stable=True)
    counts = jnp.sum(flat_e[:, None] == experts[None, :], axis=0).astype(i32)
    padded = (counts + MOE_ROWS - 1) // MOE_ROWS * MOE_ROWS
    end_pad = jnp.cumsum(padded)
    start_pad = end_pad - padded
    start = jnp.cumsum(counts) - counts
    block_start = jnp.arange(n_blocks, dtype=i32) * MOE_ROWS
    block_e = jnp.minimum(jnp.sum(block_start[:, None] >= end_pad[None, :], axis=1), N_EXPERTS - 1).astype(i32)
    off = (block_start - start_pad[block_e])[:, None] + jnp.arange(MOE_ROWS, dtype=i32)[None, :]
    src = jnp.where(off < counts[block_e][:, None], start[block_e][:, None] + off, 0).reshape(-1)
    slot_tok = order[src] // TOP_K
    if after_routing is not None:
        slot_tok = after_routing(slot_tok)
    shift = jnp.sum(jnp.where(e_sorted[:, None] == experts[None, :], (start_pad - start)[None, :], 0), axis=1)
    _, pos = lax.sort((order, iota + shift), num_keys=1)
    n_used = (end_pad[-1] // MOE_ROWS).astype(i32)
    part = n_blocks // MOE_SPLIT
    outs = []
    for s in range(MOE_SPLIT):
        rows = slice(s * part * MOE_ROWS, (s + 1) * part * MOE_ROWS)
        used = jnp.clip(n_used - s * part, 0, part).reshape(1)
        outs.append(moe_expert_blocks(block_e[s * part:(s + 1) * part], used, h2[slot_tok[rows]],
                                      w1p, b1g, b1l, w2, b2))
    out = outs[0] if MOE_SPLIT == 1 else jnp.concatenate(outs, axis=0)
    return out[pos.reshape(n_tok, TOP_K).T.reshape(-1)], weights


def _combine_kernel(*refs, final):
    p_refs = refs[:TOP_K]
    w_ref, x_ref, g2_ref = refs[TOP_K:TOP_K + 3]
    rest = refs[TOP_K + 3:]
    o_ref = rest[-1]
    w = w_ref[...]
    f = w[:, 0:1] * p_refs[0][...].astype(jnp.float32)
    for k in range(1, TOP_K):
        f = f + w[:, k:k + 1] * p_refs[k][...].astype(jnp.float32)
    xn = x_ref[...] + g2_ref[0] * f
    if final:
        ms = jnp.mean(xn * xn, axis=-1, keepdims=True)
        xn = xn * lax.rsqrt(ms + NORM_EPS) * rest[0][...]
    o_ref[...] = xn


def combine_residual(picked, weights, x2d, g2, first_row, rows_per_mod, final_gain=None):
    rows, d = x2d.shape
    n_tok = weights.shape[0]
    tm = MERGE_TM
    blocks_per_mod = rows_per_mod // tm
    first = first_row // tm
    k_blocks = n_tok // tm
    in_specs = [pl.BlockSpec((tm, d), lambda i, k=k: (k * k_blocks + first + i, 0)) for k in range(TOP_K)]
    in_specs += [pl.BlockSpec((tm, TOP_K), lambda i: (first + i, 0)),
                 pl.BlockSpec((tm, d), lambda i: (i, 0)),
                 pl.BlockSpec((1, 1, d), lambda i: (i // blocks_per_mod, 0, 0))]
    args = [picked] * TOP_K + [weights, x2d, g2]
    if final_gain is not None:
        in_specs.append(pl.BlockSpec((1, d), lambda i: (0, 0)))
        args.append(final_gain)
    return pl.pallas_call(
        functools.partial(_combine_kernel, final=final_gain is not None),
        grid=(rows // tm,),
        in_specs=in_specs,
        out_specs=pl.BlockSpec((tm, d), lambda i: (i, 0)),
        out_shape=jax.ShapeDtypeStruct((rows, d), jnp.float32),
        compiler_params=pltpu.CompilerParams(
            dimension_semantics=("parallel",),
            vmem_limit_bytes=VMEM_LIMIT_BYTES),
        name="combine_residual",
    )(*args)


def kernel(x, c, ctx, c_ctx, ada_w, ada_b, norm1_g, norm2_g, final_norm_g, w_in,
           hy_conv_w, hy_conv_b, hy_f_w1, hy_f_b1, hy_f_w2, hy_f_b2, hy_f_w3, hy_f_b3, hy_sin_freq, hy_bias,
           s5_lam_re, s5_lam_im, s5_log_dt, s5_b_re, s5_b_im, s5_c_re, s5_c_im, s5_d, s5_glu_w,
           ml_conv_w, ml_conv_b, ml_gate_b, ml_norm_g,
           w_br_hy, w_br_s5, w_br_ml, w_out,
           moe_router_w, moe_router_b, moe_w1, moe_b1, moe_w2, moe_b2):
    bsz, seq_len, d = x.shape
    ctx_len = ctx.shape[1]
    bf, f32 = jnp.bfloat16, jnp.float32
    silu_c = jax.nn.silu(c.astype(f32))
    silu_cc = jax.nn.silu(c_ctx.astype(f32))[None]
    x2 = x.reshape(bsz * seq_len, d)
    c2 = ctx.reshape(bsz * ctx_len, d)
    raw = dict(
        w_in=w_in, norm1_g=norm1_g, norm2_g=norm2_g, hy_conv_w=hy_conv_w, hy_conv_b=hy_conv_b,
        hy_f=(hy_f_w1, hy_f_b1, hy_f_w2, hy_f_b2, hy_f_w3, hy_f_b3, hy_sin_freq), hy_bias=hy_bias,
        s5=(s5_lam_re, s5_lam_im, s5_log_dt, s5_b_re, s5_b_im, s5_c_re, s5_c_im, s5_d), s5_glu_w=s5_glu_w,
        ml_conv_w=ml_conv_w, ml_conv_b=ml_conv_b, ml_gate_b=ml_gate_b, ml_norm_g=ml_norm_g,
        w_br_hy=w_br_hy, w_br_s5=w_br_s5, w_br_ml=w_br_ml, w_out=w_out,
        moe_router_w=moe_router_w, moe_router_b=moe_router_b, moe_w1=moe_w1, moe_b1=moe_b1,
        moe_w2=moe_w2, moe_b2=moe_b2)

    def prep(l, p):
        w = {}
        w["w_in"] = permute_w_in(p["w_in"][l])
        w["gain1"] = p["norm1_g"][l].reshape(1, d)
        w["n2"] = p["norm2_g"][l].reshape(1, d)
        hy_params = tuple(a[l] for a in p["hy_f"])
        w["conv_wb"] = jnp.concatenate([p["hy_conv_w"][l], p["hy_conv_b"][l][None]], axis=0)
        w["hy_bias"] = p["hy_bias"][l]
        w["fr_l"] = hyena_reversed_taps(seq_len, hy_params)
        w["fr_c"] = hyena_reversed_taps(ctx_len, hy_params) if l < DEPTH - 1 else None
        w["s5"] = s5_chunk_weights(*(a[l] for a in p["s5"]))
        cw, cb = p["ml_conv_w"][l], p["ml_conv_b"][l]
        w["ml_cw"] = jnp.concatenate([pad_heads(cw[:, :D_ML]), pad_heads(cw[:, D_ML:])], axis=1)
        w["ml_cb"] = jnp.concatenate([pad_heads(cb[:D_ML]), pad_heads(cb[D_ML:])])[None]
        w["ml_gb"] = jnp.pad(p["ml_gate_b"][l], (0, GATE_PAD - 4 * ML_HEADS))[None]
        w["ml_ng"] = pad_heads(p["ml_norm_g"][l])[None]
        w["wglu"], w["wh"] = p["s5_glu_w"][l].astype(bf), p["w_br_hy"][l].astype(bf)
        w["ws"], w["wo"] = p["w_br_s5"][l].astype(bf), p["w_out"][l].astype(bf)
        w["wm"] = pad_heads(p["w_br_ml"][l].T).T.astype(bf)
        w["rw"] = jnp.pad(p["moe_router_w"][l], ((0, 0), (0, ROUTER_PAD - N_EXPERTS))).astype(bf)
        w["rb"] = jnp.pad(p["moe_router_b"][l], (0, ROUTER_PAD - N_EXPERTS))[None]
        w1p = deinterleave_cast(p["moe_w1"].reshape(DEPTH * N_EXPERTS * d, 2 * D_FF_EXPERT), l)
        w["moe"] = (w1p.reshape(N_EXPERTS, d, 2 * D_FF_EXPERT),
                    p["moe_b1"][l][:, None, 0::2], p["moe_b1"][l][:, None, 1::2],
                    p["moe_w2"][l].astype(bf), p["moe_b2"][l][:, None, :])
        return w

    per_b = lambda a, n: a.reshape(bsz, n, a.shape[-1])
    flat = lambda a: a.reshape(-1, a.shape[-1])
    w = prep(0, raw)
    for l in range(DEPTH):
        need_ctx = l < DEPTH - 1
        mod_l = (silu_c @ ada_w[l] + ada_b[l])[:, None, :]
        mod_c = (silu_cc @ ada_w[l] + ada_b[l])[:, None, :]
        sh1_l, sc1_l, g1_l, sh2_l, sc2_l, g2_l = jnp.split(mod_l, 6, axis=-1)
        sh1_c, sc1_c, g1_c, sh2_c, sc2_c, g2_c = jnp.split(mod_c, 6, axis=-1)

        s5_l, qk_l, v_l, hy_l, o_l, mg_l, gt_l = norm_mod_project(x2, sh1_l, sc1_l, w["gain1"], w["w_in"], seq_len)
        s5_c, qk_c, v_c, hy_c, o_c, mg_c, gt_c = norm_mod_project(c2, sh1_c, sc1_c, w["gain1"], w["w_in"],
                                                                  bsz * ctx_len)
        hy_out_l = hyena_mixer(per_b(hy_l, seq_len), w["conv_wb"], w["hy_bias"], w["fr_l"],
                               ch=HY_CH, row_len=GRID_W, pad_batch=bsz)
        s5_out_l, s5_out_c = s5_scan(per_b(s5_l, seq_len), per_b(s5_c, ctx_len), w["s5"])
        ml_out_c, ml_out_l = mlstm_call(
            per_b(qk_c, ctx_len), per_b(qk_l, seq_len), per_b(v_c, ctx_len), per_b(v_l, seq_len),
            per_b(gt_c, ctx_len), per_b(gt_l, seq_len), per_b(o_c, ctx_len), per_b(o_l, seq_len),
            w["ml_cw"], w["ml_cb"], w["ml_gb"], w["ml_ng"], row_len=GRID_W)
        merge_w = (w["wglu"], w["wh"], w["ws"], w["wm"], w["wo"], w["rw"], w["rb"])
        x2, h2_l, lg_l = merge_project_residual(
            flat(hy_out_l), flat(s5_out_l), flat(ml_out_l), mg_l, x2, g1_l, sh2_l, sc2_l, w["n2"],
            *merge_w, seq_len)

        nxt = {}

        def after_routing(slot_tok, l=l, nxt=nxt):
            if l + 1 < DEPTH:
                slot_tok, p = lax.optimization_barrier((slot_tok, raw))
                nxt["w"] = prep(l + 1, p)
            return slot_tok

        if need_ctx:
            hy_out_c = hyena_mixer(per_b(hy_c, ctx_len), w["conv_wb"], w["hy_bias"], w["fr_c"],
                                   ch=ctx_len, row_len=ctx_len, pad_batch=LANES)
            c2, h2_c, lg_c = merge_project_residual(
                flat(hy_out_c), flat(s5_out_c), flat(ml_out_c), mg_c, c2, g1_c, sh2_c, sc2_c, w["n2"],
                *merge_w, bsz * ctx_len)
            tok = jnp.concatenate([h2_c, h2_l], axis=0)
            lg = jnp.concatenate([lg_c, lg_l], axis=0)
            picked, wts = moe(tok, lg[:, :N_EXPERTS], *w["moe"], after_routing=after_routing)
            c2 = combine_residual(picked, wts, c2, g2_c, 0, bsz * ctx_len)
            x2 = combine_residual(picked, wts, x2, g2_l, bsz * ctx_len, seq_len)
            w = nxt["w"]
        else:
            picked, wts = moe(h2_l, lg_l[:, :N_EXPERTS], *w["moe"], after_routing=after_routing)
            x2 = combine_residual(picked, wts, x2, g2_l, 0, seq_len, final_gain=final_norm_g.reshape(1, d))
    return x2.reshape(bsz, seq_len, d)
```

```python
import functools
import math

import jax
import jax.numpy as jnp
from jax import lax
from jax.experimental import pallas as pl
from jax.experimental.pallas import tpu as pltpu

D_MODEL = 1024
DEPTH = 4
GRID_W = 64

D_HY = 384
D_S5 = 384
D_ML = 384
N_BRANCH = 3
SHORT_CONV = 3

HY_ORDER = 2
HY_EMB = 33
HY_BANDS = (HY_EMB - 1) // 2
HY_DECAY_TARGET = 1e-2
HY_FAST_DECAY_PCT = 0.3
HY_SLOW_DECAY_PCT = 1.5
HY_MIN_DECAY = math.log(HY_DECAY_TARGET) / HY_SLOW_DECAY_PCT
HY_MAX_DECAY = math.log(HY_DECAY_TARGET) / HY_FAST_DECAY_PCT

S5_GROUP = 16
S5_GROUPS = D_S5 // S5_GROUP
S5_STATE = 64

ML_HEADS = 4
ML_HEAD_DIM = D_ML // ML_HEADS
NEG = -1e30

N_EXPERTS = 32
TOP_K = 4
D_FF_EXPERT = 512
SWIGLU_LIMIT = 7.0
SWIGLU_ALPHA = 1.702

NORM_EPS = 1e-6

COL_SIZES = (D_S5, 2 * D_ML, D_ML, 4 * ML_HEADS, (1 + HY_ORDER) * D_HY, D_ML, N_BRANCH * D_MODEL)

LANES = 128
VMEM_LIMIT_BYTES = 56 * 1024 * 1024

ML_HP = LANES
ML_DP = ML_HEADS * ML_HP
ML_TC = 256
GATE_PAD = LANES

PROJ_OUT = (("s5", D_S5, jnp.bfloat16), ("qk", 2 * ML_DP, jnp.bfloat16), ("v", ML_DP, jnp.bfloat16),
            ("hy", (1 + HY_ORDER) * D_HY, jnp.bfloat16), ("o", ML_DP, jnp.bfloat16),
            ("mg", N_BRANCH * D_MODEL, jnp.bfloat16), ("gt", GATE_PAD, jnp.float32))
PROJ_COLS = sum(w for _, w, _ in PROJ_OUT)
PROJ_TM = 512

MOE_ROWS = 512
MOE_SPLIT = 1
MERGE_TM = 512
ROUTER_PAD = LANES

HY_CB = 16
HI16 = -65536
HY_CH = 128


def _proj_kernel(x_ref, shift_ref, scale_ref, g_ref, w_ref, *o_refs):
    x = x_ref[...]
    ms = jnp.mean(x * x, axis=-1, keepdims=True)
    y = x * lax.rsqrt(ms + NORM_EPS) * g_ref[...]
    h = (y * (1.0 + scale_ref[0]) + shift_ref[0]).astype(jnp.bfloat16)
    start = 0
    for o_ref, (_, width, _) in zip(o_refs, PROJ_OUT):
        o_ref[...] = jnp.dot(h, w_ref[:, start:start + width],
                             preferred_element_type=jnp.float32).astype(o_ref.dtype)
        start += width


def norm_mod_project(x2d, shift, scale, gain, w_bf16, rows_per_mod):
    rows, d = x2d.shape
    tm = PROJ_TM
    blocks_per_mod = rows_per_mod // tm
    return pl.pallas_call(
        _proj_kernel,
        grid=(rows // tm,),
        in_specs=[
            pl.BlockSpec((tm, d), lambda i: (i, 0)),
            pl.BlockSpec((1, 1, d), lambda i: (i // blocks_per_mod, 0, 0)),
            pl.BlockSpec((1, 1, d), lambda i: (i // blocks_per_mod, 0, 0)),
            pl.BlockSpec((1, d), lambda i: (0, 0)),
            pl.BlockSpec((d, PROJ_COLS), lambda i: (0, 0)),
        ],
        out_specs=[pl.BlockSpec((tm, w), lambda i: (i, 0)) for _, w, _ in PROJ_OUT],
        out_shape=[jax.ShapeDtypeStruct((rows, w), dt) for _, w, dt in PROJ_OUT],
        compiler_params=pltpu.CompilerParams(
            dimension_semantics=("parallel",),
            vmem_limit_bytes=VMEM_LIMIT_BYTES),
        name="norm_mod_project",
    )(x2d, shift, scale, gain, w_bf16)


def pad_heads(w):
    lead = w.shape[:-1]
    w = w.reshape(lead + (ML_HEADS, ML_HEAD_DIM))
    w = jnp.pad(w, [(0, 0)] * len(lead) + [(0, 0), (0, ML_HP - ML_HEAD_DIM)])
    return w.reshape(lead + (ML_DP,))


def permute_w_in(w_in):
    cols, start = [], 0
    for s in COL_SIZES:
        cols.append(w_in[:, start:start + s])
        start += s
    s5, qk, v, gt, hy, o, mg = cols
    qk = jnp.concatenate([pad_heads(qk[:, :D_ML]), pad_heads(qk[:, D_ML:])], axis=1)
    gt = jnp.pad(gt, ((0, 0), (0, GATE_PAD - gt.shape[1])))
    return jnp.concatenate([s5, qk, pad_heads(v), hy, pad_heads(o), mg, gt], axis=1).astype(jnp.bfloat16)


def _merge_kernel(hy_ref, s5_ref, ml_ref, mg_ref, x_ref, g1_ref, sh2_ref, sc2_ref, n2_ref,
                  wglu_ref, wh_ref, ws_ref, wm_ref, wo_ref, rw_ref, rb_ref,
                  xo_ref, h2_ref, lg_ref):
    bf = jnp.bfloat16
    f32 = jnp.float32
    d = D_MODEL
    g = jax.nn.gelu(s5_ref[...].astype(f32))
    s5 = g * jax.nn.sigmoid(jnp.dot(g.astype(bf), wglu_ref[...], preferred_element_type=f32))
    y = jax.nn.sigmoid(mg_ref[:, 0:d].astype(f32)) * jnp.dot(hy_ref[...], wh_ref[...], preferred_element_type=f32)
    y = y + jax.nn.sigmoid(mg_ref[:, d:2 * d].astype(f32)) * jnp.dot(s5.astype(bf), ws_ref[...],
                                                                     preferred_element_type=f32)
    y = y + jax.nn.sigmoid(mg_ref[:, 2 * d:3 * d].astype(f32)) * jnp.dot(ml_ref[...], wm_ref[...],
                                                                         preferred_element_type=f32)
    out = jnp.dot(y.astype(bf), wo_ref[...], preferred_element_type=f32)
    xn = x_ref[...] + g1_ref[0] * out
    xo_ref[...] = xn
    ms = jnp.mean(xn * xn, axis=-1, keepdims=True)
    h2 = (xn * lax.rsqrt(ms + NORM_EPS) * n2_ref[...]) * (1.0 + sc2_ref[0]) + sh2_ref[0]
    h2b = h2.astype(bf)
    h2_ref[...] = h2b
    lg_ref[...] = jnp.dot(h2b, rw_ref[...], preferred_element_type=f32) + rb_ref[...]


def merge_project_residual(hy, s5, ml, mg, x2d, g1, sh2, sc2, n2, wglu, wh, ws, wm, wo, rw, rb, rows_per_mod):
    rows, d = x2d.shape
    tm = MERGE_TM
    blocks_per_mod = rows_per_mod // tm
    row_spec = lambda c: pl.BlockSpec((tm, c), lambda i: (i, 0))
    mod_spec = pl.BlockSpec((1, 1, d), lambda i: (i // blocks_per_mod, 0, 0))
    full = lambda a: pl.BlockSpec(a.shape, lambda i: (0, 0))
    return pl.pallas_call(
        _merge_kernel,
        grid=(rows // tm,),
        in_specs=[row_spec(D_HY), row_spec(D_S5), row_spec(ML_DP), row_spec(N_BRANCH * d), row_spec(d),
                  mod_spec, mod_spec, mod_spec, full(n2),
                  full(wglu), full(wh), full(ws), full(wm), full(wo), full(rw), full(rb)],
        out_specs=[row_spec(d), row_spec(d), row_spec(ROUTER_PAD)],
        out_shape=[jax.ShapeDtypeStruct((rows, d), jnp.float32),
                   jax.ShapeDtypeStruct((rows, d), jnp.bfloat16),
                   jax.ShapeDtypeStruct((rows, ROUTER_PAD), jnp.float32)],
        compiler_params=pltpu.CompilerParams(
            dimension_semantics=("parallel",),
            vmem_limit_bytes=VMEM_LIMIT_BYTES),
        name="merge_project_residual",
    )(hy, s5, ml, mg, x2d, g1, sh2, sc2, n2, wglu, wh, ws, wm, wo, rw, rb)


def _deinterleave_kernel(w_ref, p_ref, o_ref):
    o_ref[...] = jnp.dot(w_ref[...].astype(jnp.bfloat16), p_ref[...],
                         preferred_element_type=jnp.float32).astype(jnp.bfloat16)


def deinterleave_cast(w_all, layer):
    rows, n = w_all.shape[0] // DEPTH, w_all.shape[1]
    tm = 1024
    first = layer * (rows // tm)
    src = jnp.concatenate([jnp.arange(0, n, 2), jnp.arange(1, n, 2)])
    perm = (jnp.arange(n)[:, None] == src[None, :]).astype(jnp.bfloat16)
    return pl.pallas_call(
        _deinterleave_kernel,
        grid=(rows // tm,),
        in_specs=[pl.BlockSpec((tm, n), lambda i: (first + i, 0)), pl.BlockSpec((n, n), lambda i: (0, 0))],
        out_specs=pl.BlockSpec((tm, n), lambda i: (i, 0)),
        out_shape=jax.ShapeDtypeStruct((rows, n), jnp.bfloat16),
        compiler_params=pltpu.CompilerParams(
            dimension_semantics=("parallel",),
            vmem_limit_bytes=VMEM_LIMIT_BYTES),
        name="deinterleave_cast",
    )(w_all, perm)


def _moe_kernel(be_ref, nu_ref, x_ref, w1_ref, b1g_ref, b1l_ref, w2_ref, b2_ref, o_ref):
    f = D_FF_EXPERT

    @pl.when(pl.program_id(0) < nu_ref[0])
    def _():
        x = x_ref[...]
        hg = jnp.dot(x, w1_ref[0, :, 0:f], preferred_element_type=jnp.float32) + b1g_ref[0]
        hl = jnp.dot(x, w1_ref[0, :, f:2 * f], preferred_element_type=jnp.float32) + b1l_ref[0]
        x_glu = jnp.minimum(hg, SWIGLU_LIMIT)
        x_lin = jnp.clip(hl, -SWIGLU_LIMIT, SWIGLU_LIMIT)
        act = x_glu * jax.nn.sigmoid(SWIGLU_ALPHA * x_glu) * (x_lin + 1.0)
        out = jnp.dot(act.astype(jnp.bfloat16), w2_ref[0], preferred_element_type=jnp.float32) + b2_ref[0]
        o_ref[...] = out.astype(o_ref.dtype)


def moe_expert_blocks(block_e, n_used, x_sorted, w1p, b1g, b1l, w2, b2):
    n_slots, d = x_sorted.shape
    n_blocks = n_slots // MOE_ROWS
    f = D_FF_EXPERT

    def row_map(i, be, nu):
        return (jnp.maximum(jnp.minimum(i, nu[0] - 1), 0), 0)

    def w_map(i, be, nu):
        return (be[jnp.maximum(jnp.minimum(i, nu[0] - 1), 0)], 0, 0)

    grid_spec = pltpu.PrefetchScalarGridSpec(
        num_scalar_prefetch=2,
        grid=(n_blocks,),
        in_specs=[
            pl.BlockSpec((MOE_ROWS, d), row_map),
            pl.BlockSpec((1, d, 2 * f), w_map),
            pl.BlockSpec((1, 1, f), w_map),
            pl.BlockSpec((1, 1, f), w_map),
            pl.BlockSpec((1, f, d), w_map),
            pl.BlockSpec((1, 1, d), w_map),
        ],
        out_specs=pl.BlockSpec((MOE_ROWS, d), row_map),
    )
    return pl.pallas_call(
        _moe_kernel,
        grid_spec=grid_spec,
        out_shape=jax.ShapeDtypeStruct((n_slots, d), jnp.bfloat16),
        compiler_params=pltpu.CompilerParams(
            dimension_semantics=("arbitrary",),
            vmem_limit_bytes=VMEM_LIMIT_BYTES),
        name="moe_expert_blocks",
    )(block_e, n_used, x_sorted, w1p, b1g, b1l, w2, b2)


S5_T = 16
S5_W = S5_T * S5_GROUP
S5_HALF = 2 * S5_STATE


def _s5_kernel(u_ref, wp_ref, wm_ref, wq0_ref, wq1_ref, ar_ref, ai_ref, y_ref, s_scr, x_scr, yb_scr, ur_scr,
               *, n_batch, ctx_chunks, lat_chunks):
    nb = n_batch
    u = u_ref[0]
    segments = ((0, ctx_chunks), (ctx_chunks * nb, lat_chunks))
    for seg_start, seg_chunks in segments:
        for i in range(seg_chunks):
            dst = seg_start + nb * i
            src = seg_start + nb * (seg_chunks - 1 - i)
            ur_scr[dst:dst + nb, :] = u_ref[0, src:src + nb, :]
    s_scr[...] = (jnp.dot(u, wp_ref[0, 0:S5_W, :], preferred_element_type=jnp.float32)
                  + jnp.dot(ur_scr[...], wp_ref[0, S5_W:2 * S5_W, :], preferred_element_type=jnp.float32))
    ar = jnp.broadcast_to(ar_ref[0], (nb, S5_HALF))
    ai = jnp.broadcast_to(ai_ref[0], (nb, S5_HALF))

    def step(i, carry):
        xr, xi = carry
        r = pl.multiple_of(i * nb, nb)
        x_scr[pl.ds(r, nb), 0:S5_HALF] = xr
        x_scr[pl.ds(r, nb), S5_HALF:2 * S5_HALF] = xi
        sr = s_scr[pl.ds(r, nb), 0:S5_HALF]
        si = s_scr[pl.ds(r, nb), S5_HALF:2 * S5_HALF]
        return ar * xr - ai * xi + sr, ar * xi + ai * xr + si

    zero = jnp.zeros((nb, S5_HALF), jnp.float32)
    lax.fori_loop(0, ctx_chunks + lat_chunks, step, (zero, zero))

    xin = x_scr[...].astype(jnp.bfloat16)
    s_scr[...] = (jnp.dot(u, wm_ref[0], preferred_element_type=jnp.float32)
                  + jnp.dot(xin, wq0_ref[0], preferred_element_type=jnp.float32))
    yb_scr[...] = jnp.dot(xin, wq1_ref[0], preferred_element_type=jnp.float32)
    for seg_start, seg_chunks in segments:
        for i in range(seg_chunks):
            dst = seg_start + nb * i
            src = seg_start + nb * (seg_chunks - 1 - i)
            y_ref[0, dst:dst + nb, :] = (s_scr[dst:dst + nb, :] + yb_scr[src:src + nb, :]).astype(y_ref.dtype)


def s5_scan_call(u, wp, wm, wq0, wq1, ar, ai, *, n_batch, ctx_chunks):
    groups, rows, _ = u.shape
    lat_chunks = rows // n_batch - ctx_chunks
    per_group = lambda a: pl.BlockSpec((1,) + a.shape[1:], lambda g: (g, 0, 0))
    return pl.pallas_call(
        functools.partial(_s5_kernel, n_batch=n_batch, ctx_chunks=ctx_chunks, lat_chunks=lat_chunks),
        grid=(groups,),
        in_specs=[per_group(a) for a in (u, wp, wm, wq0, wq1, ar, ai)],
        out_specs=pl.BlockSpec((1, rows, S5_W), lambda g: (g, 0, 0)),
        out_shape=jax.ShapeDtypeStruct((groups, rows, S5_W), jnp.bfloat16),
        scratch_shapes=[pltpu.VMEM((rows, S5_W), jnp.float32)] * 3 + [pltpu.VMEM((rows, S5_W), jnp.bfloat16)],
        compiler_params=pltpu.CompilerParams(
            dimension_semantics=("parallel",),
            vmem_limit_bytes=VMEM_LIMIT_BYTES),
        name="s5_scan",
    )(u, wp, wm, wq0, wq1, ar, ai)


def s5_chunk_weights(lam_re, lam_im, log_dt, b_re, b_im, c_re, c_im, d_skip):
    f32 = jnp.float32
    t_len, g_n, p_n, n_n = S5_T, S5_GROUPS, S5_STATE, S5_GROUP
    b_mat = lax.complex(b_re.astype(f32), b_im.astype(f32))
    c_mat = lax.complex(c_re.astype(f32), c_im.astype(f32))
    lam = lax.complex(lam_re.astype(f32), lam_im.astype(f32))
    lam_dt = lam * jnp.exp(log_dt.astype(f32))[..., None]
    b_bar = ((jnp.exp(lam_dt) - 1.0) / lam)[..., None] * b_mat
    j = jnp.arange(t_len + 1, dtype=f32)
    pw = jnp.exp(j[:, None, None, None] * lam_dt[None])

    kern = [jnp.real(jnp.einsum('gnp,jgp,gpm->jgnm', c_mat, pw[:t_len, d], b_bar[d])) for d in range(2)]
    idx = jnp.arange(t_len)
    diff = idx[None, :] - idx[:, None]
    k0 = kern[0][jnp.clip(diff, 0, None)]
    k1 = kern[1][jnp.clip(-diff, 0, None)]
    m5 = (jnp.where((diff >= 0)[:, :, None, None, None], k0, 0.0)
          + jnp.where((diff <= 0)[:, :, None, None, None], k1, 0.0))
    skip = jnp.eye(t_len, dtype=f32)[:, :, None, None, None] * (
        d_skip.astype(f32).reshape(g_n, n_n)[None, None, :, :, None] * jnp.eye(n_n, dtype=f32)[None, None, None])
    wm = (m5 + skip).transpose(2, 0, 4, 1, 3).reshape(g_n, S5_W, S5_W)

    pf = jnp.einsum('sgp,gpm->gsmp', pw[t_len - 1 - idx, 0], b_bar[0]).reshape(g_n, S5_W, p_n)
    pb = jnp.einsum('sgp,gpm->gsmp', pw[idx, 1], b_bar[1]).reshape(g_n, S5_W, p_n)
    z = jnp.zeros_like(jnp.real(pf))
    wp = jnp.concatenate([
        jnp.concatenate([jnp.real(pf), z, jnp.imag(pf), z], axis=-1),
        jnp.concatenate([z, jnp.real(pb), z, jnp.imag(pb)], axis=-1)], axis=1)

    q0 = jnp.einsum('gnp,tgp->gptn', c_mat, pw[idx + 1, 0]).reshape(g_n, p_n, S5_W)
    q1 = jnp.einsum('gnp,tgp->gptn', c_mat, pw[t_len - idx, 1]).reshape(g_n, p_n, S5_W)
    zq = jnp.zeros_like(jnp.real(q0))
    wq0 = jnp.concatenate([jnp.real(q0), zq, -jnp.imag(q0), zq], axis=1)
    wq1 = jnp.concatenate([zq, jnp.real(q1), zq, -jnp.imag(q1)], axis=1)
    lam_t = pw[t_len]
    ar = jnp.concatenate([jnp.real(lam_t[0]), jnp.real(lam_t[1])], axis=-1)[:, None, :]
    ai = jnp.concatenate([jnp.imag(lam_t[0]), jnp.imag(lam_t[1])], axis=-1)[:, None, :]
    bf = jnp.bfloat16
    return wp.astype(bf), wm.astype(bf), wq0.astype(bf), wq1.astype(bf), ar, ai


def s5_scan(u_l, u_c, weights):
    bsz, seq_len, _ = u_l.shape
    ctx_len = u_c.shape[1]

    def to_chunks(a):
        a = a.reshape(bsz, a.shape[1] // S5_T, S5_T, S5_GROUPS, S5_GROUP).transpose(3, 1, 0, 2, 4)
        return a.reshape(S5_GROUPS, -1, S5_W)

    u = jnp.concatenate([to_chunks(u_c), to_chunks(u_l)], axis=1).astype(jnp.bfloat16)
    y = s5_scan_call(u, *weights, n_batch=bsz, ctx_chunks=ctx_len // S5_T)

    def from_chunks(a, length):
        a = a.reshape(S5_GROUPS, length // S5_T, bsz, S5_T, S5_GROUP)
        return a.transpose(2, 1, 3, 0, 4).reshape(bsz, length, D_S5)

    ctx_rows = ctx_len // S5_T * bsz
    return from_chunks(y[:, ctx_rows:], seq_len), from_chunks(y[:, :ctx_rows], ctx_len)


def _dot01(a01, x):
    bf, f32 = jnp.bfloat16, jnp.float32
    hi = x.astype(bf)
    r1 = x - hi.astype(f32)
    mid = r1.astype(bf)
    lo = (r1 - mid.astype(f32)).astype(bf)
    d = lambda y: jnp.dot(a01, y, preferred_element_type=f32)
    return d(hi) + d(mid) + d(lo)


def _mlstm_kernel(qkc_ref, qkl_ref, vc_ref, vl_ref, gc_ref, gl_ref, oc_ref, ol_ref,
                  cw_ref, cb_ref, gb_ref, ng_ref,
                  outc_ref, outl_ref,
                  q_scr, k_scr, v_scr, g_scr, h_scr, c_scr, n_scr, m_scr,
                  *, ctx_len, seq_len, row_len):
    bf, f32 = jnp.bfloat16, jnp.float32
    tc = ML_TC
    nc_ctx = ctx_len // tc
    nc_lat = seq_len // tc
    nc_tot = nc_ctx + nc_lat

    def conv_silu(x, n_rows, rlen):
        row = lax.broadcasted_iota(jnp.int32, (n_rows, 1), 0) % rlen
        prev = jnp.where(row == 0, 0.0, pltpu.roll(x, 1, 0))
        nxt = jnp.where(row == rlen - 1, 0.0, pltpu.roll(x, n_rows - 1, 0))
        y = cb_ref[...] + prev * cw_ref[0:1, :] + x * cw_ref[1:2, :] + nxt * cw_ref[2:3, :]
        return y * jax.nn.sigmoid(y)

    kscale = ML_HEAD_DIM ** -0.5
    a = conv_silu(qkc_ref[0].astype(f32), ctx_len, ctx_len)
    q_scr[0:ctx_len, :] = a[:, 0:ML_DP].astype(bf)
    k_scr[0:ctx_len, :] = (a[:, ML_DP:2 * ML_DP] * kscale).astype(bf)
    v_scr[0:ctx_len, :] = vc_ref[0]
    g_scr[0:ctx_len, :] = gc_ref[0] + gb_ref[...]

    def prep(c, _):
        r = pl.multiple_of(c * tc, tc)
        a = conv_silu(qkl_ref[0, pl.ds(r, tc), :].astype(f32), tc, row_len)
        q_scr[pl.ds(ctx_len + r, tc), :] = a[:, 0:ML_DP].astype(bf)
        k_scr[pl.ds(ctx_len + r, tc), :] = (a[:, ML_DP:2 * ML_DP] * kscale).astype(bf)
        v_scr[pl.ds(ctx_len + r, tc), :] = vl_ref[0, pl.ds(r, tc), :]
        g_scr[pl.ds(ctx_len + r, tc), :] = gl_ref[0, pl.ds(r, tc), :] + gb_ref[...]
        return 0

    lax.fori_loop(0, nc_lat, prep, 0)

    ti = lax.broadcasted_iota(jnp.int32, (tc, tc), 0)
    si = lax.broadcasted_iota(jnp.int32, (tc, tc), 1)

    for d in range(2):
        causal = (si <= ti) if d == 0 else (si >= ti)
        tri = causal.astype(bf)
        c_scr[...] = jnp.zeros_like(c_scr)
        n_scr[...] = jnp.zeros_like(n_scr)
        m_scr[...] = jnp.full_like(m_scr, NEG)

        def chunk(i, _, d=d, causal=causal, tri=tri):
            if d == 0:
                ci = i
            else:
                ci = jnp.where(i < nc_ctx, nc_ctx - 1 - i, nc_tot + nc_ctx - 1 - i)
            r0 = pl.multiple_of(ci * tc, tc)
            gts = g_scr[pl.ds(r0, tc), :]
            logf = jax.nn.log_sigmoid(gts)
            bm = _dot01(tri, logf)
            gts_t = gts.T
            bm_t = bm.T
            b_end_row = bm[tc - 1:tc, :] if d == 0 else bm[0:1, :]
            for h in range(ML_HEADS):
                icol = h + 8 * d
                fcol = ML_HEADS + h + 8 * d
                lanes = slice(h * ML_HP, (h + 1) * ML_HP)
                i_col = gts[:, icol:icol + 1]
                b_col = bm[:, fcol:fcol + 1]
                i_row = gts_t[icol:icol + 1, :]
                b_row = bm_t[fcol:fcol + 1, :]
                b_end = b_end_row[:, fcol:fcol + 1]
                m_prev = m_scr[h:h + 1, 0:1]
                dmat = jnp.where(causal, b_col - b_row + i_row, NEG)
                a_col = b_col + m_prev
                m_out = jnp.maximum(a_col, jnp.max(dmat, axis=1, keepdims=True))
                wmat = jnp.exp(dmat - m_out)
                wa = jnp.exp(a_col - m_out)
                q = q_scr[pl.ds(r0, tc), lanes]
                k = k_scr[pl.ds(r0, tc), lanes]
                v = v_scr[pl.ds(r0, tc), lanes]
                s = lax.dot_general(q, k, (((1,), (1,)), ((), ())), preferred_element_type=f32) * wmat
                c_mem = c_scr[h * ML_HP:(h + 1) * ML_HP, :]
                n_row = n_scr[h:h + 1, :]
                qf = q.astype(f32)
                num = (jnp.dot(s.astype(bf), v, preferred_element_type=f32)
                       + wa * lax.dot_general(q, c_mem.astype(bf), (((1,), (1,)), ((), ())),
                                              preferred_element_type=f32))
                den = jnp.sum(s, axis=1, keepdims=True) + wa * jnp.sum(qf * n_row, axis=1, keepdims=True)
                hval = num / jnp.maximum(jnp.abs(den), jnp.exp(-m_out))
                if d == 0:
                    h_scr[pl.ds(r0, tc), lanes] = hval
                else:
                    h_scr[pl.ds(r0, tc), lanes] += hval
                g_col = b_end - b_col + i_col
                m_new = jnp.maximum(b_end + m_prev, jnp.max(g_col, axis=0, keepdims=True))
                decay = jnp.exp(b_end + m_prev - m_new)
                w_col = jnp.exp(g_col - m_new)
                vw = (v.astype(f32) * w_col).astype(bf)
                c_scr[h * ML_HP:(h + 1) * ML_HP, :] = decay * c_mem + lax.dot_general(
                    vw, k, (((0,), (0,)), ((), ())), preferred_element_type=f32)
                n_scr[h:h + 1, :] = decay * n_row + jnp.sum(k.astype(f32) * w_col, axis=0, keepdims=True)
                m_scr[h:h + 1, :] = jnp.broadcast_to(m_new, (1, LANES))
            return 0

        lax.fori_loop(0, nc_tot, chunk, 0)

    def head_out(hs, o):
        outs = []
        for h in range(ML_HEADS):
            x = hs[:, h * ML_HP:(h + 1) * ML_HP]
            ms = jnp.sum(x * x, axis=1, keepdims=True) * (1.0 / ML_HEAD_DIM)
            outs.append(x * lax.rsqrt(ms + NORM_EPS))
        hn = jnp.concatenate(outs, axis=1) * ng_ref[...]
        return (hn * jax.nn.sigmoid(o.astype(f32))).astype(bf)

    outc_ref[0] = head_out(h_scr[0:ctx_len, :], oc_ref[0])

    def fin(c, _):
        r = pl.multiple_of(c * tc, tc)
        outl_ref[0, pl.ds(r, tc), :] = head_out(h_scr[pl.ds(ctx_len + r, tc), :], ol_ref[0, pl.ds(r, tc), :])
        return 0

    lax.fori_loop(0, nc_lat, fin, 0)


def mlstm_call(qk_c, qk_l, v_c, v_l, g_c, g_l, o_c, o_l, conv_w, conv_b, gate_b, norm_g, *, row_len):
    bsz, ctx_len, _ = qk_c.shape
    seq_len = qk_l.shape[1]
    tot = ctx_len + seq_len
    per_b = lambda a: pl.BlockSpec((1,) + a.shape[1:], lambda b: (b, 0, 0))
    full = lambda a: pl.BlockSpec(a.shape, lambda b: (0, 0))
    return pl.pallas_call(
        functools.partial(_mlstm_kernel, ctx_len=ctx_len, seq_len=seq_len, row_len=row_len),
        grid=(bsz,),
        in_specs=[per_b(a) for a in (qk_c, qk_l, v_c, v_l, g_c, g_l, o_c, o_l)]
        + [full(a) for a in (conv_w, conv_b, gate_b, norm_g)],
        out_specs=[pl.BlockSpec((1, ctx_len, ML_DP), lambda b: (b, 0, 0)),
                   pl.BlockSpec((1, seq_len, ML_DP), lambda b: (b, 0, 0))],
        out_shape=[jax.ShapeDtypeStruct((bsz, ctx_len, ML_DP), jnp.bfloat16),
                   jax.ShapeDtypeStruct((bsz, seq_len, ML_DP), jnp.bfloat16)],
        scratch_shapes=[pltpu.VMEM((tot, ML_DP), jnp.bfloat16)] * 3
        + [pltpu.VMEM((tot, GATE_PAD), jnp.float32), pltpu.VMEM((tot, ML_DP), jnp.float32),
           pltpu.VMEM((ML_DP, ML_HP), jnp.float32), pltpu.VMEM((8, LANES), jnp.float32),
           pltpu.VMEM((8, LANES), jnp.float32)],
        compiler_params=pltpu.CompilerParams(
            dimension_semantics=("parallel",),
            vmem_limit_bytes=VMEM_LIMIT_BYTES),
        name="mlstm",
    )(qk_c, qk_l, v_c, v_l, g_c, g_l, o_c, o_l, conv_w, conv_b, gate_b, norm_g)


def _hyena_kernel(cw_ref, hb_ref, v_ref, x1_ref, x2_ref, fr_ref, o_ref, ga_scr, gb_scr, *, ch, na, nb, row_len):
    bf, f32 = jnp.bfloat16, jnp.float32
    nl = na * nb
    cblk = pl.program_id(0)
    row = lax.broadcasted_iota(jnp.int32, (ch, 1), 0) % row_len
    first, last = row == 0, row == row_len - 1
    lane = lax.broadcasted_iota(jnp.int32, (1, nl), 1)
    below = lax.broadcasted_iota(jnp.int32, (ch, ch), 1) < lax.broadcasted_iota(jnp.int32, (ch, ch), 0)

    def short_conv(x_ref, j, part):
        x = x_ref[j].astype(f32)
        cidx = part * D_HY + cblk * HY_CB + j
        prev = jnp.where(first, 0.0, pltpu.roll(x, 1, 0))
        nxt = jnp.where(last, 0.0, pltpu.roll(x, ch - 1, 0))
        return cw_ref[3, cidx] + cw_ref[0, cidx] * prev + cw_ref[1, cidx] * x + cw_ref[2, cidx] * nxt

    def long_conv(g_ref, z):
        zp = jnp.dot(g_ref[...], z.astype(bf), preferred_element_type=f32)
        y = zp[(na - 1) * ch:na * ch]
        for p in range(1, na):
            zpos = zp[(na - 1 - p) * ch:(na - p) * ch]
            zneg = zp[(2 * na - 1 - p) * ch:(2 * na - p) * ch]
            y = y + pltpu.roll(jnp.where(lane < (na - p) * nb, zpos, zneg), p * nb, 1)
        return y

    def channel_pair(jp, _):
        js = (2 * jp, 2 * jp + 1)
        zs = [short_conv(v_ref, j, 0) for j in js]
        for o, xg_ref in ((0, x1_ref), (1, x2_ref)):
            gates = [short_conv(xg_ref, j, 1 + o) for j in js]
            taps = fr_ref[o, pl.ds(jp, 1), :]
            r_lo = pltpu.roll(jnp.broadcast_to(taps[:, 0:ch], (ch, ch)), 0, 1, stride=1, stride_axis=0)
            for q in range(2 * na - 1):
                r_hi = pltpu.roll(jnp.broadcast_to(taps[:, (q + 1) * ch:(q + 2) * ch], (ch, ch)),
                                  0, 1, stride=1, stride_axis=0)
                g = jnp.where(below, r_lo, r_hi)
                ga_scr[q * ch:(q + 1) * ch, :] = lax.bitcast_convert_type(g & HI16, f32).astype(bf)
                gb_scr[q * ch:(q + 1) * ch, :] = lax.bitcast_convert_type(g << 16, f32).astype(bf)
                r_lo = r_hi
            for i, (j, g_ref) in enumerate(zip(js, (ga_scr, gb_scr))):
                y = long_conv(g_ref, zs[i])
                zs[i] = gates[i] * (y + hb_ref[o, cblk * HY_CB + j] * zs[i])
        for i, j in enumerate(js):
            o_ref[j] = zs[i].astype(bf)
        return 0

    lax.fori_loop(0, HY_CB // 2, channel_pair, 0)


def hyena_call(conv_wb, hy_bias, ht, fr, *, ch, na, nb, row_len):
    nl = na * nb
    n_blk = D_HY // HY_CB
    part = lambda k: pl.BlockSpec((HY_CB, ch, nl), lambda i, *_: (i + k * n_blk, 0, 0))
    grid_spec = pltpu.PrefetchScalarGridSpec(
        num_scalar_prefetch=2,
        grid=(n_blk,),
        in_specs=[part(0), part(1), part(2),
                  pl.BlockSpec((HY_ORDER, HY_CB // 2, 2 * ch * na), lambda i, *_: (0, i, 0))],
        out_specs=pl.BlockSpec((HY_CB, ch, nl), lambda i, *_: (i, 0, 0)),
        scratch_shapes=[pltpu.VMEM(((2 * na - 1) * ch, ch), jnp.bfloat16)] * 2,
    )
    return pl.pallas_call(
        functools.partial(_hyena_kernel, ch=ch, na=na, nb=nb, row_len=row_len),
        grid_spec=grid_spec,
        out_shape=jax.ShapeDtypeStruct((D_HY, ch, nl), jnp.bfloat16),
        compiler_params=pltpu.CompilerParams(
            dimension_semantics=("parallel",),
            vmem_limit_bytes=VMEM_LIMIT_BYTES),
        name="hyena",
    )(conv_wb, hy_bias, ht, ht, ht, fr)


def hyena_filters(seq_len, w1, b1, w2, b2, w3, b3, sin_freq):
    t = jnp.linspace(0.0, 1.0, seq_len, dtype=jnp.float32)[:, None]
    w = 2.0 * math.pi * jnp.arange(seq_len, dtype=jnp.float32)[:, None] / seq_len
    f = jnp.linspace(1e-4, HY_BANDS - 1, HY_BANDS, dtype=jnp.float32)[None, :]
    z = jnp.concatenate([t, jnp.cos(f * w), -jnp.sin(f * w)], axis=-1)
    hdn = jnp.sin(sin_freq[0] * (z @ w1 + b1))
    hdn = jnp.sin(sin_freq[1] * (hdn @ w2 + b2))
    hf = (hdn @ w3 + b3).astype(jnp.float32).reshape(seq_len, HY_ORDER, 2, D_HY)
    deltas = jnp.abs(jnp.linspace(HY_MIN_DECAY, HY_MAX_DECAY, D_HY, dtype=jnp.float32))
    decay = jnp.exp(-t * deltas)
    return hf * decay[:, None, None, :]


def hyena_reversed_taps(seq_len, filter_params):
    filt = hyena_filters(seq_len, *filter_params)
    out = []
    for o in range(HY_ORDER):
        h_fwd, h_bwd = filt[:, o, 0], filt[:, o, 1]
        k = jnp.concatenate([h_fwd, jnp.zeros_like(h_fwd[:1]), h_bwd[:0:-1]], axis=0)
        k = k / jnp.sum(jnp.abs(k), axis=0, keepdims=True)
        idx = (seq_len - jnp.arange(2 * seq_len)) % (2 * seq_len)
        out.append(k[idx].T)
    bits = lax.bitcast_convert_type(jnp.stack(out).astype(jnp.bfloat16), jnp.uint16).astype(jnp.uint32)
    bits = bits.reshape(HY_ORDER, D_HY // 2, 2, 2 * seq_len)
    return lax.bitcast_convert_type((bits[:, :, 0] << 16) | bits[:, :, 1], jnp.int32)


def hyena_mixer(hy, conv_wb, hy_bias, fr, *, ch, row_len, pad_batch):
    bsz, seq_len, width = hy.shape
    na = seq_len // ch
    nb = pad_batch
    ht = hy.reshape(bsz, na, ch, width).transpose(3, 2, 1, 0)
    if nb != bsz:
        ht = jnp.pad(ht, ((0, 0), (0, 0), (0, 0), (0, nb - bsz)))
    out = hyena_call(conv_wb, hy_bias, ht.reshape(width, ch, na * nb), fr, ch=ch, na=na, nb=nb, row_len=row_len)
    out = out.reshape(D_HY, ch, na, nb)[..., :bsz]
    return out.transpose(3, 2, 1, 0).reshape(bsz, seq_len, D_HY)


def moe(h2, logits, w1p, b1g, b1l, w2, b2, after_routing=None):
    n_tok, d = h2.shape
    i32 = jnp.int32
    top_val, top_idx = lax.top_k(logits, TOP_K)
    weights = jax.nn.softmax(top_val, axis=-1)
    flat_e = top_idx.reshape(-1).astype(i32)
    n_assign = flat_e.shape[0]
    n_blocks = -(-n_assign // MOE_ROWS) + N_EXPERTS
    iota = jnp.arange(n_assign, dtype=i32)
    experts = jnp.arange(N_EXPERTS, dtype=i32)
    e_sorted, order = lax.sort((flat_e, iota), num_keys=1, is_stable=True)
    counts = jnp.sum(flat_e[:, None] == experts[None, :], axis=0).astype(i32)
    padded = (counts + MOE_ROWS - 1) // MOE_ROWS * MOE_ROWS
    end_pad = jnp.cumsum(padded)
    start_pad = end_pad - padded
    start = jnp.cumsum(counts) - counts
    block_start = jnp.arange(n_blocks, dtype=i32) * MOE_ROWS
    block_e = jnp.minimum(jnp.sum(block_start[:, None] >= end_pad[None, :], axis=1), N_EXPERTS - 1).astype(i32)
    off = (block_start - start_pad[block_e])[:, None] + jnp.arange(MOE_ROWS, dtype=i32)[None, :]
    valid = (off < counts[block_e][:, None]).reshape(-1)
    src = jnp.where(valid, (start[block_e][:, None] + off).reshape(-1), 0)
    slot_tok = jnp.where(valid, order[src] // TOP_K, jnp.arange(n_blocks * MOE_ROWS, dtype=i32) % n_tok)
    if after_routing is not None:
        slot_tok = after_routing(slot_tok)
    shift = jnp.sum(jnp.where(e_sorted[:, None] == experts[None, :], (start_pad - start)[None, :], 0), axis=1)
    _, pos = lax.sort((order, iota + shift), num_keys=1)
    n_used = (end_pad[-1] // MOE_ROWS).astype(i32)
    part = n_blocks // MOE_SPLIT
    outs = []
    for s in range(MOE_SPLIT):
        rows = slice(s * part * MOE_ROWS, (s + 1) * part * MOE_ROWS)
        used = jnp.clip(n_used - s * part, 0, part).reshape(1)
        outs.append(moe_expert_blocks(block_e[s * part:(s + 1) * part], used, h2[slot_tok[rows]],
                                      w1p, b1g, b1l, w2, b2))
    out = outs[0] if MOE_SPLIT == 1 else jnp.concatenate(outs, axis=0)
    return out[pos.reshape(n_tok, TOP_K).T.reshape(-1)], weights


def _combine_kernel(*refs, final):
    p_refs = refs[:TOP_K]
    w_ref, x_ref, g2_ref = refs[TOP_K:TOP_K + 3]
    rest = refs[TOP_K + 3:]
    o_ref = rest[-1]
    w = w_ref[...]
    f = w[:, 0:1] * p_refs[0][...].astype(jnp.float32)
    for k in range(1, TOP_K):
        f = f + w[:, k:k + 1] * p_refs[k][...].astype(jnp.float32)
    xn = x_ref[...] + g2_ref[0] * f
    if final:
        ms = jnp.mean(xn * xn, axis=-1, keepdims=True)
        xn = xn * lax.rsqrt(ms + NORM_EPS) * rest[0][...]
    o_ref[...] = xn


def combine_residual(picked, weights, x2d, g2, first_row, rows_per_mod, final_gain=None):
    rows, d = x2d.shape
    n_tok = weights.shape[0]
    tm = MERGE_TM
    blocks_per_mod = rows_per_mod // tm
    first = first_row // tm
    k_blocks = n_tok // tm
    in_specs = [pl.BlockSpec((tm, d), lambda i, k=k: (k * k_blocks + first + i, 0)) for k in range(TOP_K)]
    in_specs += [pl.BlockSpec((tm, TOP_K), lambda i: (first + i, 0)),
                 pl.BlockSpec((tm, d), lambda i: (i, 0)),
                 pl.BlockSpec((1, 1, d), lambda i: (i // blocks_per_mod, 0, 0))]
    args = [picked] * TOP_K + [weights, x2d, g2]
    if final_gain is not None:
        in_specs.append(pl.BlockSpec((1, d), lambda i: (0, 0)))
        args.append(final_gain)
    return pl.pallas_call(
        functools.partial(_combine_kernel, final=final_gain is not None),
        grid=(rows // tm,),
        in_specs=in_specs,
        out_specs=pl.BlockSpec((tm, d), lambda i: (i, 0)),
        out_shape=jax.ShapeDtypeStruct((rows, d), jnp.float32),
        compiler_params=pltpu.CompilerParams(
            dimension_semantics=("parallel",),
            vmem_limit_bytes=VMEM_LIMIT_BYTES),
        name="combine_residual",
    )(*args)


def kernel(x, c, ctx, c_ctx, ada_w, ada_b, norm1_g, norm2_g, final_norm_g, w_in,
           hy_conv_w, hy_conv_b, hy_f_w1, hy_f_b1, hy_f_w2, hy_f_b2, hy_f_w3, hy_f_b3, hy_sin_freq, hy_bias,
           s5_lam_re, s5_lam_im, s5_log_dt, s5_b_re, s5_b_im, s5_c_re, s5_c_im, s5_d, s5_glu_w,
           ml_conv_w, ml_conv_b, ml_gate_b, ml_norm_g,
           w_br_hy, w_br_s5, w_br_ml, w_out,
           moe_router_w, moe_router_b, moe_w1, moe_b1, moe_w2, moe_b2):
    bsz, seq_len, d = x.shape
    ctx_len = ctx.shape[1]
    bf, f32 = jnp.bfloat16, jnp.float32
    silu_c = jax.nn.silu(c.astype(f32))
    silu_cc = jax.nn.silu(c_ctx.astype(f32))[None]
    x2 = x.reshape(bsz * seq_len, d)
    c2 = ctx.reshape(bsz * ctx_len, d)
    raw = dict(
        w_in=w_in, norm1_g=norm1_g, norm2_g=norm2_g, hy_conv_w=hy_conv_w, hy_conv_b=hy_conv_b,
        hy_f=(hy_f_w1, hy_f_b1, hy_f_w2, hy_f_b2, hy_f_w3, hy_f_b3, hy_sin_freq), hy_bias=hy_bias,
        s5=(s5_lam_re, s5_lam_im, s5_log_dt, s5_b_re, s5_b_im, s5_c_re, s5_c_im, s5_d), s5_glu_w=s5_glu_w,
        ml_conv_w=ml_conv_w, ml_conv_b=ml_conv_b, ml_gate_b=ml_gate_b, ml_norm_g=ml_norm_g,
        w_br_hy=w_br_hy, w_br_s5=w_br_s5, w_br_ml=w_br_ml, w_out=w_out,
        moe_router_w=moe_router_w, moe_router_b=moe_router_b, moe_w1=moe_w1, moe_b1=moe_b1,
        moe_w2=moe_w2, moe_b2=moe_b2)

    def prep(l, p):
        w = {}
        w["w_in"] = permute_w_in(p["w_in"][l])
        w["gain1"] = p["norm1_g"][l].reshape(1, d)
        w["n2"] = p["norm2_g"][l].reshape(1, d)
        hy_params = tuple(a[l] for a in p["hy_f"])
        w["conv_wb"] = jnp.concatenate([p["hy_conv_w"][l], p["hy_conv_b"][l][None]], axis=0)
        w["hy_bias"] = p["hy_bias"][l]
        w["fr_l"] = hyena_reversed_taps(seq_len, hy_params)
        w["fr_c"] = hyena_reversed_taps(ctx_len, hy_params) if l < DEPTH - 1 else None
        w["s5"] = s5_chunk_weights(*(a[l] for a in p["s5"]))
        cw, cb = p["ml_conv_w"][l], p["ml_conv_b"][l]
        w["ml_cw"] = jnp.concatenate([pad_heads(cw[:, :D_ML]), pad_heads(cw[:, D_ML:])], axis=1)
        w["ml_cb"] = jnp.concatenate([pad_heads(cb[:D_ML]), pad_heads(cb[D_ML:])])[None]
        w["ml_gb"] = jnp.pad(p["ml_gate_b"][l], (0, GATE_PAD - 4 * ML_HEADS))[None]
        w["ml_ng"] = pad_heads(p["ml_norm_g"][l])[None]
        w["wglu"], w["wh"] = p["s5_glu_w"][l].astype(bf), p["w_br_hy"][l].astype(bf)
        w["ws"], w["wo"] = p["w_br_s5"][l].astype(bf), p["w_out"][l].astype(bf)
        w["wm"] = pad_heads(p["w_br_ml"][l].T).T.astype(bf)
        w["rw"] = jnp.pad(p["moe_router_w"][l], ((0, 0), (0, ROUTER_PAD - N_EXPERTS))).astype(bf)
        w["rb"] = jnp.pad(p["moe_router_b"][l], (0, ROUTER_PAD - N_EXPERTS))[None]
        w1p = deinterleave_cast(p["moe_w1"].reshape(DEPTH * N_EXPERTS * d, 2 * D_FF_EXPERT), l)
        w["moe"] = (w1p.reshape(N_EXPERTS, d, 2 * D_FF_EXPERT),
                    p["moe_b1"][l][:, None, 0::2], p["moe_b1"][l][:, None, 1::2],
                    p["moe_w2"][l].astype(bf), p["moe_b2"][l][:, None, :])
        return w

    per_b = lambda a, n: a.reshape(bsz, n, a.shape[-1])
    flat = lambda a: a.reshape(-1, a.shape[-1])
    w = prep(0, raw)
    for l in range(DEPTH):
        need_ctx = l < DEPTH - 1
        mod_l = (silu_c @ ada_w[l] + ada_b[l])[:, None, :]
        mod_c = (silu_cc @ ada_w[l] + ada_b[l])[:, None, :]
        sh1_l, sc1_l, g1_l, sh2_l, sc2_l, g2_l = jnp.split(mod_l, 6, axis=-1)
        sh1_c, sc1_c, g1_c, sh2_c, sc2_c, g2_c = jnp.split(mod_c, 6, axis=-1)

        s5_l, qk_l, v_l, hy_l, o_l, mg_l, gt_l = norm_mod_project(x2, sh1_l, sc1_l, w["gain1"], w["w_in"], seq_len)
        s5_c, qk_c, v_c, hy_c, o_c, mg_c, gt_c = norm_mod_project(c2, sh1_c, sc1_c, w["gain1"], w["w_in"],
                                                                  bsz * ctx_len)
        hy_out_l = hyena_mixer(per_b(hy_l, seq_len), w["conv_wb"], w["hy_bias"], w["fr_l"],
                               ch=HY_CH, row_len=GRID_W, pad_batch=bsz)
        s5_out_l, s5_out_c = s5_scan(per_b(s5_l, seq_len), per_b(s5_c, ctx_len), w["s5"])
        ml_out_c, ml_out_l = mlstm_call(
            per_b(qk_c, ctx_len), per_b(qk_l, seq_len), per_b(v_c, ctx_len), per_b(v_l, seq_len),
            per_b(gt_c, ctx_len), per_b(gt_l, seq_len), per_b(o_c, ctx_len), per_b(o_l, seq_len),
            w["ml_cw"], w["ml_cb"], w["ml_gb"], w["ml_ng"], row_len=GRID_W)
        merge_w = (w["wglu"], w["wh"], w["ws"], w["wm"], w["wo"], w["rw"], w["rb"])
        x2, h2_l, lg_l = merge_project_residual(
            flat(hy_out_l), flat(s5_out_l), flat(ml_out_l), mg_l, x2, g1_l, sh2_l, sc2_l, w["n2"],
            *merge_w, seq_len)

        nxt = {}

        def after_routing(slot_tok, l=l, nxt=nxt):
            if l + 1 < DEPTH:
                nxt["w"] = prep(l + 1, raw)
            return slot_tok

        if need_ctx:
            hy_out_c = hyena_mixer(per_b(hy_c, ctx_len), w["conv_wb"], w["hy_bias"], w["fr_c"],
                                   ch=ctx_len, row_len=ctx_len, pad_batch=LANES)
            c2, h2_c, lg_c = merge_project_residual(
                flat(hy_out_c), flat(s5_out_c), flat(ml_out_c), mg_c, c2, g1_c, sh2_c, sc2_c, w["n2"],
                *merge_w, bsz * ctx_len)
            tok = jnp.concatenate([h2_c, h2_l], axis=0)
            lg = jnp.concatenate([lg_c, lg_l], axis=0)
            picked, wts = moe(tok, lg[:, :N_EXPERTS], *w["moe"], after_routing=after_routing)
            c2 = combine_residual(picked, wts, c2, g2_c, 0, bsz * ctx_len)
            x2 = combine_residual(picked, wts, x2, g2_l, bsz * ctx_len, seq_len)
            w = nxt["w"]
        else:
            picked, wts = moe(h2_l, lg_l[:, :N_EXPERTS], *w["moe"], after_routing=after_routing)
            x2 = combine_residual(picked, wts, x2, g2_l, 0, seq_len, final_gain=final_norm_g.reshape(1, d))
    return x2.reshape(bsz, seq_len, d)
```
